```python
import math
import jax, jax.numpy as jnp
from jax import lax
import numpy as np

D_MODEL = 2048
BATCH = 2
SEQ = 8192
DEPTH = 1

HEAD_DIM = 128
DIL_GROUPS = ((128, 1), (512, 4), (2048, 16))
N_DIL_GROUPS = 3
HEADS_PER_DIL = 4
N_HEADS_A = N_DIL_GROUPS * HEADS_PER_DIL
N_Q_B = 8
N_KV_B = 2
Q_PER_KV = N_Q_B // N_KV_B
GRID_W = 64
ROPE_THETA = 10000.0
ALIBI_MAX = 8.0
D_FF = 5632
N_MOD = 9
Q_BLOCK = 128
EPS = 1e-6

D_A = N_HEADS_A * HEAD_DIM
D_A_OUT = HEADS_PER_DIL * HEAD_DIM
D_QB = N_Q_B * HEAD_DIM
D_KVB = N_KV_B * HEAD_DIM
SPLIT_POINTS = (D_A, 2 * D_A, 3 * D_A,
                3 * D_A + D_QB, 3 * D_A + D_QB + D_KVB, 3 * D_A + D_QB + 2 * D_KVB,
                3 * D_A + D_QB + 2 * D_KVB + D_MODEL)
D_IN = 3 * D_A + D_QB + 2 * D_KVB + 2 * D_MODEL

kernel_name = "hybrid_dilated_axial_gqa_macaron_adaln"


def rms_norm(x, g):
    xf = x.astype(jnp.float32)
    y = xf * lax.rsqrt(jnp.mean(xf * xf, axis=-1, keepdims=True) + EPS)
    return (y * g.astype(jnp.float32)).astype(x.dtype)


def modulate(x, shift, scale):
    return x * (1.0 + scale[:, None, :]) + shift[:, None, :]


def swiglu(x, w1, w3, w2):
    return (jax.nn.silu(x @ w1) * (x @ w3)) @ w2


def axial_rope_tables(s):
    rows = s // GRID_W
    row = jnp.repeat(jnp.arange(rows), GRID_W).astype(jnp.float32)
    col = jnp.tile(jnp.arange(GRID_W), rows).astype(jnp.float32)
    half = HEAD_DIM // 2
    inv_freq = ROPE_THETA ** (-jnp.arange(0, half, 2, dtype=jnp.float32) / half)
    ang = jnp.concatenate([row[:, None] * inv_freq, col[:, None] * inv_freq], axis=-1)
    return jnp.cos(ang), jnp.sin(ang)


def apply_rope(x, cos, sin):
    xf = x.astype(jnp.float32)
    half = HEAD_DIM // 2
    x1, x2 = xf[..., :half], xf[..., half:]
    c = cos[None, :, None, :]
    s = sin[None, :, None, :]
    return jnp.concatenate([x1 * c - x2 * s, x1 * s + x2 * c], axis=-1).astype(x.dtype)


def dilated_attention(qa, ka, va, slopes):
    b, s = qa.shape[0], qa.shape[1]
    nblk = s // Q_BLOCK
    scale = HEAD_DIM ** -0.5
    k_groups = [ka[:, :, g] for g in range(N_DIL_GROUPS)]
    v_groups = [va[:, :, g] for g in range(N_DIL_GROUPS)]

    def block(i):
        start = i * Q_BLOCK
        t = start + jnp.arange(Q_BLOCK)
        q_blk = lax.dynamic_slice_in_dim(qa, start, Q_BLOCK, axis=1)
        outs, lses = [], []
        for g, (window, dil) in enumerate(DIL_GROUPS):
            half = window // (2 * dil)
            n_keys = 2 * half + 1
            offs = dil * jnp.arange(-half, half + 1)
            idx = t[:, None] + offs[None, :]
            valid = (idx >= 0) & (idx < s)
            flat = jnp.clip(idx, 0, s - 1).reshape(-1)
            k_sel = jnp.take(k_groups[g], flat, axis=1).reshape(b, Q_BLOCK, n_keys, HEADS_PER_DIL, HEAD_DIM)
            v_sel = jnp.take(v_groups[g], flat, axis=1).reshape(b, Q_BLOCK, n_keys, HEADS_PER_DIL, HEAD_DIM)
            scores = jnp.einsum('bqhd,bqjhd->bhqj', q_blk[:, :, g], k_sel,
                                preferred_element_type=jnp.float32) * scale
            scores = scores - slopes[g][None, :, None, None] * jnp.abs(offs).astype(jnp.float32)[None, None, None, :]
            scores = jnp.where(valid[None, None], scores, -jnp.inf)
            lse = jax.nn.logsumexp(scores, axis=-1)
            p = jnp.exp(scores - lse[..., None])
            o = jnp.einsum('bhqj,bqjhd->bqhd', p.astype(va.dtype), v_sel,
                           preferred_element_type=jnp.float32)
            outs.append(o)
            lses.append(lse)
        lse_all = jnp.stack(lses, axis=0)
        wts = jax.nn.softmax(lse_all, axis=0)
        wts = jnp.transpose(wts, (0, 1, 3, 2))[..., None]
        o = jnp.sum(wts * jnp.stack(outs, axis=0), axis=0)
        return o.astype(qa.dtype)

    o = lax.map(block, jnp.arange(nblk))
    return jnp.transpose(o, (1, 0, 2, 3, 4)).reshape(b, s, D_A_OUT)


def gqa_attention(qb, kb, vb):
    b, s = qb.shape[0], qb.shape[1]
    nblk = s // Q_BLOCK
    scale = HEAD_DIM ** -0.5
    q_blocks = jnp.moveaxis(qb.reshape(b, nblk, Q_BLOCK, N_KV_B, Q_PER_KV, HEAD_DIM), 1, 0)

    def block(q_blk):
        scores = jnp.einsum('bqkgd,bskd->bkgqs', q_blk, kb,
                            preferred_element_type=jnp.float32) * scale
        p = jax.nn.softmax(scores, axis=-1)
        return jnp.einsum('bkgqs,bskd->bqkgd', p.astype(vb.dtype), vb)

    o = lax.map(block, q_blocks)
    return jnp.moveaxis(o, 0, 1).reshape(b, s, D_QB)


def token_mixing(u, w_in, q_norm_a, k_norm_a, q_norm_b, k_norm_b,
                 w_branch_a, w_branch_b, w_out, cos, sin, slopes):
    b, s, _ = u.shape
    proj = u @ w_in
    qa, ka, va, qb, kb, vb, ga, gb = jnp.split(proj, SPLIT_POINTS, axis=-1)
    qa = rms_norm(qa.reshape(b, s, N_DIL_GROUPS, HEADS_PER_DIL, HEAD_DIM), q_norm_a)
    ka = rms_norm(ka.reshape(b, s, N_DIL_GROUPS, HEADS_PER_DIL, HEAD_DIM), k_norm_a)
    va = va.reshape(b, s, N_DIL_GROUPS, HEADS_PER_DIL, HEAD_DIM)
    out_a = dilated_attention(qa, ka, va, slopes)
    qb = apply_rope(rms_norm(qb.reshape(b, s, N_Q_B, HEAD_DIM), q_norm_b), cos, sin)
    kb = apply_rope(rms_norm(kb.reshape(b, s, N_KV_B, HEAD_DIM), k_norm_b), cos, sin)
    vb = vb.reshape(b, s, N_KV_B, HEAD_DIM)
    out_b = gqa_attention(qb.reshape(b, s, N_KV_B, Q_PER_KV, HEAD_DIM), kb, vb)
    merged = jax.nn.sigmoid(ga) * (out_a @ w_branch_a) + jax.nn.sigmoid(gb) * (out_b @ w_branch_b)
    return merged @ w_out


def setup_inputs(seed: int = 0) -> dict:
    key = jax.random.key(seed)
    ks = jax.random.split(key, 24)
    f32 = jnp.float32

    def dense(k, shape, fan_in, mult=1.0):
        return jax.random.normal(k, shape, f32) * (mult * fan_in ** -0.5)

    def gain(k, shape):
        return 1.0 + 0.02 * jax.random.normal(k, shape, f32)

    L = DEPTH
    return {
        "x": jax.random.normal(ks[0], (BATCH, SEQ, D_MODEL), f32),
        "c": jax.random.normal(ks[1], (BATCH, D_MODEL), f32),
        "w_ada": dense(ks[2], (L, D_MODEL, N_MOD * D_MODEL), D_MODEL, 0.5),
        "b_ada": 0.01 * jax.random.normal(ks[3], (L, N_MOD * D_MODEL), f32),
        "norm_ffn1": gain(ks[4], (L, D_MODEL)),
        "w1_ffn1": dense(ks[5], (L, D_MODEL, D_FF), D_MODEL),
        "w3_ffn1": dense(ks[6], (L, D_MODEL, D_FF), D_MODEL),
        "w2_ffn1": dense(ks[7], (L, D_FF, D_MODEL), D_FF),
        "norm_mix": gain(ks[8], (L, D_MODEL)),
        "w_in": dense(ks[9], (L, D_MODEL, D_IN), D_MODEL),
        "q_norm_a": gain(ks[10], (L, HEAD_DIM)),
        "k_norm_a": gain(ks[11], (L, HEAD_DIM)),
        "q_norm_b": gain(ks[12], (L, HEAD_DIM)),
        "k_norm_b": gain(ks[13], (L, HEAD_DIM)),
        "w_branch_a": dense(ks[14], (L, D_A_OUT, D_MODEL), D_A_OUT),
        "w_branch_b": dense(ks[15], (L, D_QB, D_MODEL), D_QB),
        "w_out": dense(ks[16], (L, D_MODEL, D_MODEL), D_MODEL),
        "norm_ffn2": gain(ks[17], (L, D_MODEL)),
        "w1_ffn2": dense(ks[18], (L, D_MODEL, D_FF), D_MODEL),
        "w3_ffn2": dense(ks[19], (L, D_MODEL, D_FF), D_MODEL),
        "w2_ffn2": dense(ks[20], (L, D_FF, D_MODEL), D_FF),
        "norm_final": gain(ks[21], (D_MODEL,)),
    }


def reference(x, c, w_ada, b_ada, norm_ffn1, w1_ffn1, w3_ffn1, w2_ffn1, norm_mix, w_in,
              q_norm_a, k_norm_a, q_norm_b, k_norm_b, w_branch_a, w_branch_b, w_out,
              norm_ffn2, w1_ffn2, w3_ffn2, w2_ffn2, norm_final):
    s = x.shape[1]
    cos, sin = axial_rope_tables(s)
    slopes = jnp.exp2(-ALIBI_MAX * jnp.arange(1, N_HEADS_A + 1, dtype=jnp.float32) / N_HEADS_A)
    slopes = slopes.reshape(N_DIL_GROUPS, HEADS_PER_DIL)
    c_act = jax.nn.silu(c)
    h = x
    for l in range(DEPTH):
        mod = c_act @ w_ada[l] + b_ada[l]
        sh1, sc1, g1, sh2, sc2, g2, sh3, sc3, g3 = jnp.split(mod, N_MOD, axis=-1)
        u = modulate(rms_norm(h, norm_ffn1[l]), sh1, sc1)
        h = h + 0.5 * g1[:, None, :] * swiglu(u, w1_ffn1[l], w3_ffn1[l], w2_ffn1[l])
        u = modulate(rms_norm(h, norm_mix[l]), sh2, sc2)
        h = h + g2[:, None, :] * token_mixing(u, w_in[l], q_norm_a[l], k_norm_a[l], q_norm_b[l],
                                              k_norm_b[l], w_branch_a[l], w_branch_b[l], w_out[l],
                                              cos, sin, slopes)
        u = modulate(rms_norm(h, norm_ffn2[l]), sh3, sc3)
        h = h + 0.5 * g3[:, None, :] * swiglu(u, w1_ffn2[l], w3_ffn2[l], w2_ffn2[l])
    return rms_norm(h, norm_final)
```

```python
import functools

import jax
import jax.numpy as jnp
from jax import lax
from jax.experimental import pallas as pl
from jax.experimental.pallas import tpu as pltpu

F32 = jnp.float32
BF16 = jnp.bfloat16

D_MODEL = 2048
D_FF = 5632
HEAD_DIM = 128
DIL_GROUPS = ((128, 1), (512, 4), (2048, 16))
N_DIL_GROUPS = 3
HEADS_PER_DIL = 4
N_HEADS_A = N_DIL_GROUPS * HEADS_PER_DIL
N_Q_B = 8
N_KV_B = 2
Q_PER_KV = N_Q_B // N_KV_B
GRID_W = 64
ROPE_THETA = 10000.0
ALIBI_MAX = 8.0
N_MOD = 9
EPS = 1e-6
D_A = N_HEADS_A * HEAD_DIM
D_A_OUT = HEADS_PER_DIL * HEAD_DIM
D_QB = N_Q_B * HEAD_DIM
D_KVB = N_KV_B * HEAD_DIM
SM_SCALE = HEAD_DIM ** -0.5
MASK_VALUE = -1e30

LANES = 128
VMEM_LIMIT = 56 * 1024 * 1024


def _params(sem):
    return pltpu.CompilerParams(dimension_semantics=sem, vmem_limit_bytes=VMEM_LIMIT)


def _silu(x):
    return x * jax.nn.sigmoid(x)


def _rms(x, gain):
    ms = jnp.mean(x * x, axis=-1, keepdims=True)
    return x * lax.rsqrt(ms + EPS) * gain


ADA_TN = 1024


def _ada_kernel(c_ref, w_ref, b_ref, o_ref):
    n_b = c_ref.shape[0]
    for b in range(n_b):
        cs = _silu(c_ref[b])
        for j in range(ADA_TN // LANES):
            sl = slice(j * LANES, (j + 1) * LANES)
            acc = jnp.sum(w_ref[:, sl] * cs, axis=0, keepdims=True)
            o_ref[b, :, sl] = acc + b_ref[:, sl]


def _ada(c, w, b):
    n_b, d = c.shape
    n = w.shape[1]
    c_b = jnp.broadcast_to(c[:, :, None], (n_b, d, LANES))
    return pl.pallas_call(
        _ada_kernel,
        grid=(n // ADA_TN,),
        in_specs=[
            pl.BlockSpec((n_b, d, LANES), lambda j: (0, 0, 0)),
            pl.BlockSpec((d, ADA_TN), lambda j: (0, j)),
            pl.BlockSpec((1, ADA_TN), lambda j: (0, j)),
        ],
        out_specs=pl.BlockSpec((n_b, 1, ADA_TN), lambda j: (0, 0, j)),
        out_shape=jax.ShapeDtypeStruct((n_b, 1, n), F32),
        compiler_params=_params(("arbitrary",)),
        name="ada_mod",
    )(c_b, w, b.reshape(1, n))


FFN_TM = 512
FFN_TF = 512


def _ffn_kernel(x_ref, gain_ref, sh_ref, sc_ref, g_ref, w1_ref, w3_ref, w2_ref, *rest, mode):
    if mode == "mid":
        gain2_ref, sh2_ref, sc2_ref, h_ref, u2_ref, u_scr, acc_scr = rest
    else:
        gainf_ref, y_ref, u_scr, acc_scr = rest
    f = pl.program_id(2)
    nf = pl.num_programs(2)

    @pl.when(f == 0)
    def _():
        y = _rms(x_ref[0], gain_ref[...])
        u_scr[...] = (y * (1.0 + sc_ref[0, 0]) + sh_ref[0, 0]).astype(BF16)

    u = u_scr[...]
    a = jnp.dot(u, w1_ref[...], preferred_element_type=F32)
    b = jnp.dot(u, w3_ref[...], preferred_element_type=F32)
    hm = (_silu(a) * b).astype(BF16)
    contrib = jnp.dot(hm, w2_ref[...], preferred_element_type=F32)

    @pl.when(f == 0)
    def _():
        acc_scr[...] = contrib

    @pl.when(f > 0)
    def _():
        acc_scr[...] += contrib

    @pl.when(f == nf - 1)
    def _():
        h = x_ref[0] + (0.5 * g_ref[0, 0]) * acc_scr[...]
        if mode == "mid":
            h_ref[0] = h
            y2 = _rms(h, gain2_ref[...])
            u2_ref[0] = (y2 * (1.0 + sc2_ref[0, 0]) + sh2_ref[0, 0]).astype(BF16)
        else:
            y_ref[0] = _rms(h, gainf_ref[...])


def _mod_spec(k):
    return pl.BlockSpec((1, 1, 1, D_MODEL), lambda b, i, f, k=k: (b, k, 0, 0))


def _ffn(x, mod4, gain, w1, w3, w2, ks, mode, gain_next, ks_next=None):
    n_b, s, d = x.shape
    tm, tf = FFN_TM, FFN_TF
    grid = (n_b, s // tm, D_FF // tf)
    row_spec = pl.BlockSpec((1, tm, d), lambda b, i, f: (b, i, 0))
    vec_spec = pl.BlockSpec((1, d), lambda b, i, f: (0, 0))
    in_specs = [
        row_spec, vec_spec, _mod_spec(ks[0]), _mod_spec(ks[1]), _mod_spec(ks[2]),
        pl.BlockSpec((d, tf), lambda b, i, f: (0, f)),
        pl.BlockSpec((d, tf), lambda b, i, f: (0, f)),
        pl.BlockSpec((tf, d), lambda b, i, f: (f, 0)),
        vec_spec,
    ]
    args = [x, gain.reshape(1, d), mod4, mod4, mod4, w1, w3, w2, gain_next.reshape(1, d)]
    if mode == "mid":
        in_specs += [_mod_spec(ks_next[0]), _mod_spec(ks_next[1])]
        args += [mod4, mod4]
        out_specs = [row_spec, row_spec]
        out_shape = [jax.ShapeDtypeStruct((n_b, s, d), F32), jax.ShapeDtypeStruct((n_b, s, d), BF16)]
    else:
        out_specs = row_spec
        out_shape = jax.ShapeDtypeStruct((n_b, s, d), F32)
    return pl.pallas_call(
        functools.partial(_ffn_kernel, mode=mode),
        grid=grid,
        in_specs=in_specs,
        out_specs=out_specs,
        out_shape=out_shape,
        scratch_shapes=[pltpu.VMEM((tm, d), BF16), pltpu.VMEM((tm, d), F32)],
        compiler_params=_params(("parallel", "parallel", "arbitrary")),
        name="ffn_" + mode,
    )(*args)


PROJ_TM = 1024


def _proj_kernel(u_ref, w_ref, *rest, mode, tn, scale):
    if mode == "norm":
        gain_ref, o_ref = rest
    elif mode == "norm_rope":
        gain_ref, cos_ref, sin_ref, o_ref = rest
    else:
        (o_ref,) = rest
    acc = jnp.dot(u_ref[0], w_ref[...], preferred_element_type=F32)
    if mode == "plain":
        o_ref[0] = acc.astype(BF16)
    elif mode == "sigmoid":
        o_ref[0] = jax.nn.sigmoid(acc).astype(BF16)
    else:
        for j in range(tn // HEAD_DIM):
            sl = slice(j * HEAD_DIM, (j + 1) * HEAD_DIM)
            y = _rms(acc[:, sl], gain_ref[:, sl])
            if mode == "norm_rope":
                y = y * cos_ref[...] + pltpu.roll(y, HEAD_DIM // 2, axis=1) * sin_ref[...]
            if scale != 1.0:
                y = y * scale
            o_ref[0, :, sl] = y.astype(BF16)


def _proj(u, w, mode, tn, gain_row=None, cos=None, sin=None, scale=1.0):
    n_b, s, d = u.shape
    n = w.shape[1]
    tm = PROJ_TM
    in_specs = [
        pl.BlockSpec((1, tm, d), lambda b, i, j: (b, i, 0)),
        pl.BlockSpec((d, tn), lambda b, i, j: (0, j)),
    ]
    args = [u, w]
    if mode in ("norm", "norm_rope"):
        in_specs.append(pl.BlockSpec((1, tn), lambda b, i, j: (0, j)))
        args.append(gain_row)
    if mode == "norm_rope":
        in_specs += [pl.BlockSpec((tm, HEAD_DIM), lambda b, i, j: (i, 0))] * 2
        args += [cos, sin]
    return pl.pallas_call(
        functools.partial(_proj_kernel, mode=mode, tn=tn, scale=scale),
        grid=(n_b, s // tm, n // tn),
        in_specs=in_specs,
        out_specs=pl.BlockSpec((1, tm, tn), lambda b, i, j: (b, i, j)),
        out_shape=jax.ShapeDtypeStruct((n_b, s, n), BF16),
        compiler_params=_params(("parallel", "parallel", "arbitrary")),
        name="proj_" + mode,
    )(*args)


GQA_TQ = 256
GQA_TK = 512


def _gqa_kernel(q_ref, k_ref, v_ref, o_ref):
    s_len = k_ref.shape[1]
    tq = q_ref.shape[1]
    for g in range(Q_PER_KV):
        sl = slice(g * HEAD_DIM, (g + 1) * HEAD_DIM)
        q = q_ref[0, :, sl]

        def body(c, carry, q=q):
            m, l, acc = carry
            start = pl.multiple_of(c * GQA_TK, GQA_TK)
            k = k_ref[0, pl.ds(start, GQA_TK), :]
            v = v_ref[0, pl.ds(start, GQA_TK), :]
            s = lax.dot_general(q, k, (((1,), (1,)), ((), ())), preferred_element_type=F32)
            m_new = jnp.maximum(m, jnp.max(s, axis=-1, keepdims=True))
            alpha = jnp.exp(m - m_new)
            p = jnp.exp(s - m_new)
            l = alpha * l + jnp.sum(p, axis=-1, keepdims=True)
            acc = alpha * acc + jnp.dot(p.astype(BF16), v, preferred_element_type=F32)
            return m_new, l, acc

        init = (jnp.full((tq, 1), MASK_VALUE, F32), jnp.zeros((tq, 1), F32),
                jnp.zeros((tq, HEAD_DIM), F32))
        _, l, acc = lax.fori_loop(0, s_len // GQA_TK, body, init)
        o_ref[0, :, sl] = (acc / l).astype(BF16)


def _gqa(qb, kb, vb):
    n_b, s, _ = qb.shape
    tq = GQA_TQ
    w = Q_PER_KV * HEAD_DIM
    return pl.pallas_call(
        _gqa_kernel,
        grid=(n_b, N_KV_B, s // tq),
        in_specs=[
            pl.BlockSpec((1, tq, w), lambda b, h, i: (b, i, h)),
            pl.BlockSpec((1, s, HEAD_DIM), lambda b, h, i: (b, 0, h)),
            pl.BlockSpec((1, s, HEAD_DIM), lambda b, h, i: (b, 0, h)),
        ],
        out_specs=pl.BlockSpec((1, tq, w), lambda b, h, i: (b, i, h)),
        out_shape=jax.ShapeDtypeStruct((n_b, s, D_QB), BF16),
        compiler_params=_params(("parallel", "parallel", "arbitrary")),
        name="gqa_attn",
    )(qb, kb, vb)


DIL_TQ = 128
DIL_HALF = 64


def _dil_kernel(q_ref, kp_ref, kc_ref, kn_ref, vp_ref, vc_ref, vn_ref, o_ref, lse_ref, *, group, dil, n_rows):
    i = pl.program_id(2)
    tq = DIL_TQ
    q_pos = i * tq + lax.broadcasted_iota(jnp.int32, (tq, 3 * tq), 0)
    k_pos = (i - 1) * tq + lax.broadcasted_iota(jnp.int32, (tq, 3 * tq), 1)
    dist = jnp.abs(k_pos - q_pos)
    valid = (dist <= DIL_HALF) & (k_pos >= 0) & (k_pos < n_rows)
    dist_f = dist.astype(F32) * float(dil)
    for h in range(HEADS_PER_DIL):
        sl = slice(h * HEAD_DIM, (h + 1) * HEAD_DIM)
        slope = 2.0 ** (-ALIBI_MAX * (group * HEADS_PER_DIL + h + 1) / N_HEADS_A)
        k = jnp.concatenate([kp_ref[0, :, sl], kc_ref[0, :, sl], kn_ref[0, :, sl]], axis=0)
        v = jnp.concatenate([vp_ref[0, :, sl], vc_ref[0, :, sl], vn_ref[0, :, sl]], axis=0)
        s = lax.dot_general(q_ref[0, :, sl], k, (((1,), (1,)), ((), ())), preferred_element_type=F32)
        s = jnp.where(valid, s - slope * dist_f, MASK_VALUE)
        m = jnp.max(s, axis=-1, keepdims=True)
        p = jnp.exp(s - m)
        l = jnp.sum(p, axis=-1, keepdims=True)
        o = jnp.dot(p.astype(BF16), v, preferred_element_type=F32)
        o_ref[0, :, sl] = o / l
        lse_ref[0, :, sl] = jnp.broadcast_to(m + jnp.log(l), (tq, HEAD_DIM))


def _dilated_group(qa, ka, va, group, dil):
    n_b, s, _ = qa.shape
    n_rows = s // dil
    tq = DIL_TQ
    nblk = n_rows // tq
    w = D_A_OUT
    qr = qa.reshape(n_b, n_rows, dil * D_A)
    kr = ka.reshape(n_b, n_rows, dil * D_A)
    vr = va.reshape(n_b, n_rows, dil * D_A)
    col = lambda r: r * N_DIL_GROUPS + group

    def kv_spec(shift):
        return pl.BlockSpec(
            (1, tq, w), lambda b, r, i: (b, jnp.clip(i + shift, 0, nblk - 1), col(r)))

    o, lse = pl.pallas_call(
        functools.partial(_dil_kernel, group=group, dil=dil, n_rows=n_rows),
        grid=(n_b, dil, nblk),
        in_specs=[pl.BlockSpec((1, tq, w), lambda b, r, i: (b, i, col(r))),
                  kv_spec(-1), kv_spec(0), kv_spec(1), kv_spec(-1), kv_spec(0), kv_spec(1)],
        out_specs=[pl.BlockSpec((1, tq, w), lambda b, r, i: (b, i, r))] * 2,
        out_shape=[jax.ShapeDtypeStruct((n_b, n_rows, dil * w), F32)] * 2,
        compiler_params=_params(("parallel", "parallel", "arbitrary")),
        name="dilated_attn_g%d" % group,
    )(qr, kr, kr, kr, vr, vr, vr)
    return o.reshape(n_b, s, w), lse.reshape(n_b, s, w)


MERGE_TM = 256


def _merge_kernel(o0_ref, o1_ref, o2_ref, l0_ref, l1_ref, l2_ref, ob_ref, sa_ref, sb_ref,
                  wa_ref, wb_ref, wo_ref, h_ref, g_ref, out_ref):
    l0, l1, l2 = l0_ref[0], l1_ref[0], l2_ref[0]
    mx = jnp.maximum(jnp.maximum(l0, l1), l2)
    e0, e1, e2 = jnp.exp(l0 - mx), jnp.exp(l1 - mx), jnp.exp(l2 - mx)
    oa = (e0 * o0_ref[0] + e1 * o1_ref[0] + e2 * o2_ref[0]) / (e0 + e1 + e2)
    pa = jnp.dot(oa.astype(BF16), wa_ref[...], preferred_element_type=F32)
    pb = jnp.dot(ob_ref[0], wb_ref[...], preferred_element_type=F32)
    merged = sa_ref[0].astype(F32) * pa + sb_ref[0].astype(F32) * pb
    mixed = jnp.dot(merged.astype(BF16), wo_ref[...], preferred_element_type=F32)
    out_ref[0] = h_ref[0] + g_ref[0, 0] * mixed


def _merge(o_groups, lse_groups, out_b, gates, wa, wb, wo, h, mod4, k_gate):
    n_b, s, d = h.shape
    tm = MERGE_TM
    a_spec = pl.BlockSpec((1, tm, D_A_OUT), lambda b, i: (b, i, 0))
    row_spec = pl.BlockSpec((1, tm, d), lambda b, i: (b, i, 0))
    full = lambda shape: pl.BlockSpec(shape, lambda b, i: (0, 0))
    return pl.pallas_call(
        _merge_kernel,
        grid=(n_b, s // tm),
        in_specs=[a_spec] * 6 + [
            pl.BlockSpec((1, tm, D_QB), lambda b, i: (b, i, 0)),
            pl.BlockSpec((1, tm, d), lambda b, i: (b, i, 0)),
            pl.BlockSpec((1, tm, d), lambda b, i: (b, i, 1)),
            full((D_A_OUT, d)), full((D_QB, d)), full((d, d)),
            row_spec,
            pl.BlockSpec((1, 1, 1, d), lambda b, i: (b, k_gate, 0, 0)),
        ],
        out_specs=row_spec,
        out_shape=jax.ShapeDtypeStruct((n_b, s, d), F32),
        compiler_params=_params(("parallel", "arbitrary")),
        name="mix_merge",
    )(*o_groups, *lse_groups, out_b, gates, gates, wa, wb, wo, h, mod4)


def _rope_tables(s):
    rows = s // GRID_W
    row = jnp.repeat(jnp.arange(rows), GRID_W).astype(F32)
    col = jnp.tile(jnp.arange(GRID_W), rows).astype(F32)
    half = HEAD_DIM // 2
    inv_freq = ROPE_THETA ** (-jnp.arange(0, half, 2, dtype=F32) / half)
    ang = jnp.concatenate([row[:, None] * inv_freq, col[:, None] * inv_freq], axis=-1)
    cos, sin = jnp.cos(ang), jnp.sin(ang)
    return jnp.concatenate([cos, cos], axis=-1), jnp.concatenate([-sin, sin], axis=-1)


def kernel(x, c, w_ada, b_ada, norm_ffn1, w1_ffn1, w3_ffn1, w2_ffn1, norm_mix, w_in, q_norm_a, k_norm_a, q_norm_b, k_norm_b, w_branch_a, w_branch_b, w_out, norm_ffn2, w1_ffn2, w3_ffn2, w2_ffn2, norm_final):
    n_b, s, d = x.shape
    assert w_ada.shape[0] == 1, "the fused pipeline covers a single layer"
    cos_f, sin_f = _rope_tables(s)
    h = x
    for l in range(1):
        mod4 = _ada(c, w_ada[l], b_ada[l]).reshape(n_b, N_MOD, 1, d)
        h, u = _ffn(h, mod4, norm_ffn1[l], w1_ffn1[l].astype(BF16), w3_ffn1[l].astype(BF16),
                    w2_ffn1[l].astype(BF16), (0, 1, 2), "mid", norm_mix[l], (3, 4))
        wi = w_in[l]
        o_qk = 3 * D_A
        gain_a = jnp.concatenate([jnp.tile(q_norm_a[l], N_HEADS_A), jnp.tile(k_norm_a[l], N_HEADS_A)])[None]
        gain_b = jnp.concatenate([jnp.tile(q_norm_b[l], N_Q_B), jnp.tile(k_norm_b[l], N_KV_B)])[None]
        qa = _proj(u, wi[:, :D_A].astype(BF16), "norm", 512, gain_a[:, :D_A], scale=SM_SCALE)
        ka = _proj(u, wi[:, D_A:2 * D_A].astype(BF16), "norm", 512, gain_a[:, D_A:])
        va = _proj(u, wi[:, 2 * D_A:o_qk].astype(BF16), "plain", 512)
        qb = _proj(u, wi[:, o_qk:o_qk + D_QB].astype(BF16), "norm_rope", 512, gain_b[:, :D_QB],
                   cos_f, sin_f, scale=SM_SCALE)
        kb = _proj(u, wi[:, o_qk + D_QB:o_qk + D_QB + D_KVB].astype(BF16), "norm_rope", 256,
                   gain_b[:, D_QB:], cos_f, sin_f)
        vb = _proj(u, wi[:, o_qk + D_QB + D_KVB:o_qk + D_QB + 2 * D_KVB].astype(BF16), "plain", 256)
        gates = _proj(u, wi[:, o_qk + D_QB + 2 * D_KVB:].astype(BF16), "sigmoid", 512)

        o_groups, lse_groups = [], []
        for g, (_, dil) in enumerate(DIL_GROUPS):
            o_g, lse_g = _dilated_group(qa, ka, va, g, dil)
            o_groups.append(o_g)
            lse_groups.append(lse_g)
        out_b = _gqa(qb, kb, vb)
        h = _merge(o_groups, lse_groups, out_b, gates, w_branch_a[l].astype(BF16),
                   w_branch_b[l].astype(BF16), w_out[l].astype(BF16), h, mod4, 5)
        h = _ffn(h, mod4, norm_ffn2[l], w1_ffn2[l].astype(BF16), w3_ffn2[l].astype(BF16),
                 w2_ffn2[l].astype(BF16), (6, 7, 8), "final", norm_final)
    return h
```

```python
import functools

import jax
import jax.numpy as jnp
from jax import lax
from jax.experimental import pallas as pl
from jax.experimental.pallas import tpu as pltpu

F32 = jnp.float32
BF16 = jnp.bfloat16

D_MODEL = 2048
D_FF = 5632
HEAD_DIM = 128
DIL_GROUPS = ((128, 1), (512, 4), (2048, 16))
N_DIL_GROUPS = 3
HEADS_PER_DIL = 4
N_HEADS_A = N_DIL_GROUPS * HEADS_PER_DIL
N_Q_B = 8
N_KV_B = 2
Q_PER_KV = N_Q_B // N_KV_B
GRID_W = 64
ROPE_THETA = 10000.0
ALIBI_MAX = 8.0
N_MOD = 9
EPS = 1e-6
D_A = N_HEADS_A * HEAD_DIM
D_A_OUT = HEADS_PER_DIL * HEAD_DIM
D_QB = N_Q_B * HEAD_DIM
D_KVB = N_KV_B * HEAD_DIM
SM_SCALE = HEAD_DIM ** -0.5
MASK_VALUE = -1e30

LANES = 128
VMEM_LIMIT = 56 * 1024 * 1024


def _params(sem):
    return pltpu.CompilerParams(dimension_semantics=sem, vmem_limit_bytes=VMEM_LIMIT)


def _silu(x):
    return x * jax.nn.sigmoid(x)


def _rms(x, gain):
    ms = jnp.mean(x * x, axis=-1, keepdims=True)
    return x * lax.rsqrt(ms + EPS) * gain


ADA_TN = 1024


def _ada_kernel(c_ref, w_ref, b_ref, o_ref):
    n_b = c_ref.shape[0]
    for b in range(n_b):
        cs = _silu(c_ref[b])
        for j in range(ADA_TN // LANES):
            sl = slice(j * LANES, (j + 1) * LANES)
            acc = jnp.sum(w_ref[:, sl] * cs, axis=0, keepdims=True)
            o_ref[b, :, sl] = acc + b_ref[:, sl]


def _ada(c, w, b):
    n_b, d = c.shape
    n = w.shape[1]
    c_b = jnp.broadcast_to(c[:, :, None], (n_b, d, LANES))
    return pl.pallas_call(
        _ada_kernel,
        grid=(n // ADA_TN,),
        in_specs=[
            pl.BlockSpec((n_b, d, LANES), lambda j: (0, 0, 0)),
            pl.BlockSpec((d, ADA_TN), lambda j: (0, j)),
            pl.BlockSpec((1, ADA_TN), lambda j: (0, j)),
        ],
        out_specs=pl.BlockSpec((n_b, 1, ADA_TN), lambda j: (0, 0, j)),
        out_shape=jax.ShapeDtypeStruct((n_b, 1, n), F32),
        compiler_params=_params(("arbitrary",)),
        name="ada_mod",
    )(c_b, w, b.reshape(1, n))


FFN_TM = 512
FFN_TF = 512


def _ffn_kernel(x_ref, gain_ref, sh_ref, sc_ref, g_ref, w1_ref, w3_ref, w2_ref, *rest, mode):
    if mode == "mid":
        gain2_ref, sh2_ref, sc2_ref, h_ref, u2_ref, u_scr, acc_scr = rest
    else:
        gainf_ref, y_ref, u_scr, acc_scr = rest
    f = pl.program_id(2)
    nf = pl.num_programs(2)

    @pl.when(f == 0)
    def _():
        y = _rms(x_ref[0], gain_ref[...])
        u_scr[...] = (y * (1.0 + sc_ref[0, 0]) + sh_ref[0, 0]).astype(BF16)

    u = u_scr[...]
    a = jnp.dot(u, w1_ref[...], preferred_element_type=F32)
    b = jnp.dot(u, w3_ref[...], preferred_element_type=F32)
    hm = (_silu(a) * b).astype(BF16)
    contrib = jnp.dot(hm, w2_ref[...], preferred_element_type=F32)

    @pl.when(f == 0)
    def _():
        acc_scr[...] = contrib

    @pl.when(f > 0)
    def _():
        acc_scr[...] += contrib

    @pl.when(f == nf - 1)
    def _():
        h = x_ref[0] + (0.5 * g_ref[0, 0]) * acc_scr[...]
        if mode == "mid":
            h_ref[0] = h
            y2 = _rms(h, gain2_ref[...])
            u2_ref[0] = (y2 * (1.0 + sc2_ref[0, 0]) + sh2_ref[0, 0]).astype(BF16)
        else:
            y_ref[0] = _rms(h, gainf_ref[...])


def _mod_spec(k):
    return pl.BlockSpec((1, 1, 1, D_MODEL), lambda b, i, f, k=k: (b, k, 0, 0))


def _ffn(x, mod4, gain, w1, w3, w2, ks, mode, gain_next, ks_next=None):
    n_b, s, d = x.shape
    tm, tf = FFN_TM, FFN_TF
    grid = (n_b, s // tm, D_FF // tf)
    row_spec = pl.BlockSpec((1, tm, d), lambda b, i, f: (b, i, 0))
    vec_spec = pl.BlockSpec((1, d), lambda b, i, f: (0, 0))
    in_specs = [
        row_spec, vec_spec, _mod_spec(ks[0]), _mod_spec(ks[1]), _mod_spec(ks[2]),
        pl.BlockSpec((d, tf), lambda b, i, f: (0, f)),
        pl.BlockSpec((d, tf), lambda b, i, f: (0, f)),
        pl.BlockSpec((tf, d), lambda b, i, f: (f, 0)),
        vec_spec,
    ]
    args = [x, gain.reshape(1, d), mod4, mod4, mod4, w1, w3, w2, gain_next.reshape(1, d)]
    if mode == "mid":
        in_specs += [_mod_spec(ks_next[0]), _mod_spec(ks_next[1])]
        args += [mod4, mod4]
        out_specs = [row_spec, row_spec]
        out_shape = [jax.ShapeDtypeStruct((n_b, s, d), F32), jax.ShapeDtypeStruct((n_b, s, d), BF16)]
    else:
        out_specs = row_spec
        out_shape = jax.ShapeDtypeStruct((n_b, s, d), F32)
    return pl.pallas_call(
        functools.partial(_ffn_kernel, mode=mode),
        grid=grid,
        in_specs=in_specs,
        out_specs=out_specs,
        out_shape=out_shape,
        scratch_shapes=[pltpu.VMEM((tm, d), BF16), pltpu.VMEM((tm, d), F32)],
        compiler_params=_params(("parallel", "parallel", "arbitrary")),
        name="ffn_" + mode,
    )(*args)


PROJ_TM = 1024


def _proj_kernel(u_ref, w_ref, *rest, mode, tn, scale):
    if mode == "norm":
        gain_ref, o_ref = rest
    elif mode == "norm_rope":
        gain_ref, cos_ref, sin_ref, o_ref = rest
    else:
        (o_ref,) = rest
    acc = jnp.dot(u_ref[0], w_ref[...], preferred_element_type=F32)
    if mode == "plain":
        o_ref[0] = acc.astype(BF16)
    elif mode == "chunk_t":
        acc_t = acc.T.astype(BF16)
        for hh in range(tn // HEAD_DIM):
            for cc in range(acc.shape[0] // GQA_TK):
                o_ref[0, hh, cc] = acc_t[hh * HEAD_DIM:(hh + 1) * HEAD_DIM, cc * GQA_TK:(cc + 1) * GQA_TK]
    elif mode == "sigmoid":
        o_ref[0] = jax.nn.sigmoid(acc).astype(BF16)
    else:
        for j in range(tn // HEAD_DIM):
            sl = slice(j * HEAD_DIM, (j + 1) * HEAD_DIM)
            y = _rms(acc[:, sl], gain_ref[:, sl])
            if mode == "norm_rope":
                y = y * cos_ref[...] + pltpu.roll(y, HEAD_DIM // 2, axis=1) * sin_ref[...]
            if scale != 1.0:
                y = y * scale
            o_ref[0, :, sl] = y.astype(BF16)


def _proj(u, w, mode, tn, gain_row=None, cos=None, sin=None, scale=1.0):
    n_b, s, d = u.shape
    n = w.shape[1]
    tm = PROJ_TM
    in_specs = [
        pl.BlockSpec((1, tm, d), lambda b, i, j: (b, i, 0)),
        pl.BlockSpec((d, tn), lambda b, i, j: (0, j)),
    ]
    args = [u, w]
    if mode in ("norm", "norm_rope"):
        in_specs.append(pl.BlockSpec((1, tn), lambda b, i, j: (0, j)))
        args.append(gain_row)
    if mode == "norm_rope":
        in_specs += [pl.BlockSpec((tm, HEAD_DIM), lambda b, i, j: (i, 0))] * 2
        args += [cos, sin]
    if mode == "chunk_t":
        assert n == tn
        heads = n // HEAD_DIM
        out_spec = pl.BlockSpec((1, heads, tm // GQA_TK, HEAD_DIM, GQA_TK), lambda b, i, j: (b, 0, i, 0, 0))
        out_shape = jax.ShapeDtypeStruct((n_b, heads, s // GQA_TK, HEAD_DIM, GQA_TK), BF16)
    else:
        out_spec = pl.BlockSpec((1, tm, tn), lambda b, i, j: (b, i, j))
        out_shape = jax.ShapeDtypeStruct((n_b, s, n), BF16)
    return pl.pallas_call(
        functools.partial(_proj_kernel, mode=mode, tn=tn, scale=scale),
        grid=(n_b, s // tm, n // tn),
        in_specs=in_specs,
        out_specs=out_spec,
        out_shape=out_shape,
        compiler_params=_params(("parallel", "parallel", "arbitrary")),
        name="proj_" + mode,
    )(*args)


GQA_TQ = 256
GQA_TK = 512
LOG2E = 1.4426950408889634


def _gqa_kernel(q_ref, k_ref, vt_ref, o_ref, acc_ref, s_ref):
    n_chunks = vt_ref.shape[2]
    tq = q_ref.shape[1]
    heads = range(Q_PER_KV)
    nt_dims = (((1,), (1,)), ((), ()))

    def k_chunk(c):
        return k_ref[0, pl.ds(pl.multiple_of(c * GQA_TK, GQA_TK), GQA_TK), :]

    def q_head(g):
        return q_ref[0, :, g * HEAD_DIM:(g + 1) * HEAD_DIM]

    def step(c, c_next, slot, ms, ls):
        k_next = k_chunk(c_next)
        vt = vt_ref[0, 0, c]
        new_ms, new_ls = [], []
        for g in heads:
            s_ref[1 - slot, g] = lax.dot_general(k_next, q_head(g), nt_dims, preferred_element_type=F32)
            m_new = jnp.maximum(ms[g], jnp.max(s_ref[slot, g], axis=0, keepdims=True))
            alpha = jnp.exp2(ms[g] - m_new)
            p = jnp.exp2(s_ref[slot, g] - m_new)
            new_ls.append(alpha * ls[g] + jnp.sum(p, axis=0, keepdims=True))
            new_ms.append(m_new)
            acc_ref[g] = alpha * acc_ref[g] + jnp.dot(vt, p.astype(BF16), preferred_element_type=F32)
        return tuple(new_ms), tuple(new_ls)

    acc_ref[...] = jnp.zeros(acc_ref.shape, F32)
    k0 = k_chunk(0)
    for g in heads:
        s_ref[0, g] = lax.dot_general(k0, q_head(g), nt_dims, preferred_element_type=F32)

    def body(i, carry):
        ms, ls = carry
        ms, ls = step(2 * i, 2 * i + 1, 0, ms, ls)
        return step(2 * i + 1, jnp.minimum(2 * i + 2, n_chunks - 1), 1, ms, ls)

    init = (tuple(jnp.full((1, tq), MASK_VALUE, F32) for _ in heads),
            tuple(jnp.zeros((1, tq), F32) for _ in heads))
    _, ls = lax.fori_loop(0, n_chunks // 2, body, init)
    for g in heads:
        o_ref[0, :, g * HEAD_DIM:(g + 1) * HEAD_DIM] = (acc_ref[g] / ls[g]).T.astype(BF16)


def _gqa(qb, kb, vbt):
    n_b, s, _ = qb.shape
    tq = GQA_TQ
    w = Q_PER_KV * HEAD_DIM
    return pl.pallas_call(
        _gqa_kernel,
        grid=(n_b, N_KV_B, s // tq),
        in_specs=[
            pl.BlockSpec((1, tq, w), lambda b, h, i: (b, i, h)),
            pl.BlockSpec((1, s, HEAD_DIM), lambda b, h, i: (b, 0, h)),
            pl.BlockSpec((1, 1, s // GQA_TK, HEAD_DIM, GQA_TK), lambda b, h, i: (b, h, 0, 0, 0)),
        ],
        out_specs=pl.BlockSpec((1, tq, w), lambda b, h, i: (b, i, h)),
        out_shape=jax.ShapeDtypeStruct((n_b, s, D_QB), BF16),
        scratch_shapes=[pltpu.VMEM((Q_PER_KV, HEAD_DIM, tq), F32),
                        pltpu.VMEM((2, Q_PER_KV, GQA_TK, tq), F32)],
        compiler_params=_params(("parallel", "parallel", "arbitrary")),
        name="gqa_attn",
    )(qb, kb, vbt)


DIL_TQ = 128
DIL_HALF = 64


def _dil_kernel(q_ref, kp_ref, kc_ref, kn_ref, vp_ref, vc_ref, vn_ref, o_ref, lse_ref, *, group, dil, n_rows):
    i = pl.program_id(2)
    tq = DIL_TQ
    q_pos = i * tq + lax.broadcasted_iota(jnp.int32, (tq, 3 * tq), 0)
    k_pos = (i - 1) * tq + lax.broadcasted_iota(jnp.int32, (tq, 3 * tq), 1)
    dist = jnp.abs(k_pos - q_pos)
    valid = (dist <= DIL_HALF) & (k_pos >= 0) & (k_pos < n_rows)
    dist_f = dist.astype(F32) * float(dil)
    for h in range(HEADS_PER_DIL):
        sl = slice(h * HEAD_DIM, (h + 1) * HEAD_DIM)
        slope = 2.0 ** (-ALIBI_MAX * (group * HEADS_PER_DIL + h + 1) / N_HEADS_A)
        k = jnp.concatenate([kp_ref[0, :, sl], kc_ref[0, :, sl], kn_ref[0, :, sl]], axis=0)
        v = jnp.concatenate([vp_ref[0, :, sl], vc_ref[0, :, sl], vn_ref[0, :, sl]], axis=0)
        s = lax.dot_general(q_ref[0, :, sl], k, (((1,), (1,)), ((), ())), preferred_element_type=F32)
        s = jnp.where(valid, s - slope * dist_f, MASK_VALUE)
        m = jnp.max(s, axis=-1, keepdims=True)
        p = jnp.exp(s - m)
        l = jnp.sum(p, axis=-1, keepdims=True)
        o = jnp.dot(p.astype(BF16), v, preferred_element_type=F32)
        o_ref[0, :, sl] = o / l
        lse_ref[0, :, sl] = jnp.broadcast_to(m + jnp.log(l), (tq, HEAD_DIM))


def _dilated_group(qa, ka, va, group, dil):
    n_b, s, _ = qa.shape
    n_rows = s // dil
    tq = DIL_TQ
    nblk = n_rows // tq
    w = D_A_OUT
    qr = qa.reshape(n_b, n_rows, dil * D_A)
    kr = ka.reshape(n_b, n_rows, dil * D_A)
    vr = va.reshape(n_b, n_rows, dil * D_A)
    col = lambda r: r * N_DIL_GROUPS + group

    def kv_spec(shift):
        return pl.BlockSpec(
            (1, tq, w), lambda b, r, i: (b, jnp.clip(i + shift, 0, nblk - 1), col(r)))

    o, lse = pl.pallas_call(
        functools.partial(_dil_kernel, group=group, dil=dil, n_rows=n_rows),
        grid=(n_b, dil, nblk),
        in_specs=[pl.BlockSpec((1, tq, w), lambda b, r, i: (b, i, col(r))),
                  kv_spec(-1), kv_spec(0), kv_spec(1), kv_spec(-1), kv_spec(0), kv_spec(1)],
        out_specs=[pl.BlockSpec((1, tq, w), lambda b, r, i: (b, i, r))] * 2,
        out_shape=[jax.ShapeDtypeStruct((n_b, n_rows, dil * w), F32)] * 2,
        compiler_params=_params(("parallel", "parallel", "arbitrary")),
        name="dilated_attn_g%d" % group,
    )(qr, kr, kr, kr, vr, vr, vr)
    return o.reshape(n_b, s, w), lse.reshape(n_b, s, w)


MERGE_TM = 256


def _merge_kernel(o0_ref, o1_ref, o2_ref, l0_ref, l1_ref, l2_ref, ob_ref, sa_ref, sb_ref,
                  wa_ref, wb_ref, wo_ref, h_ref, g_ref, out_ref):
    l0, l1, l2 = l0_ref[0], l1_ref[0], l2_ref[0]
    mx = jnp.maximum(jnp.maximum(l0, l1), l2)
    e0, e1, e2 = jnp.exp(l0 - mx), jnp.exp(l1 - mx), jnp.exp(l2 - mx)
    oa = (e0 * o0_ref[0] + e1 * o1_ref[0] + e2 * o2_ref[0]) / (e0 + e1 + e2)
    pa = jnp.dot(oa.astype(BF16), wa_ref[...], preferred_element_type=F32)
    pb = jnp.dot(ob_ref[0], wb_ref[...], preferred_element_type=F32)
    merged = sa_ref[0].astype(F32) * pa + sb_ref[0].astype(F32) * pb
    mixed = jnp.dot(merged.astype(BF16), wo_ref[...], preferred_element_type=F32)
    out_ref[0] = h_ref[0] + g_ref[0, 0] * mixed


def _merge(o_groups, lse_groups, out_b, gates, wa, wb, wo, h, mod4, k_gate):
    n_b, s, d = h.shape
    tm = MERGE_TM
    a_spec = pl.BlockSpec((1, tm, D_A_OUT), lambda b, i: (b, i, 0))
    row_spec = pl.BlockSpec((1, tm, d), lambda b, i: (b, i, 0))
    full = lambda shape: pl.BlockSpec(shape, lambda b, i: (0, 0))
    return pl.pallas_call(
        _merge_kernel,
        grid=(n_b, s // tm),
        in_specs=[a_spec] * 6 + [
            pl.BlockSpec((1, tm, D_QB), lambda b, i: (b, i, 0)),
            pl.BlockSpec((1, tm, d), lambda b, i: (b, i, 0)),
            pl.BlockSpec((1, tm, d), lambda b, i: (b, i, 1)),
            full((D_A_OUT, d)), full((D_QB, d)), full((d, d)),
            row_spec,
            pl.BlockSpec((1, 1, 1, d), lambda b, i: (b, k_gate, 0, 0)),
        ],
        out_specs=row_spec,
        out_shape=jax.ShapeDtypeStruct((n_b, s, d), F32),
        compiler_params=_params(("parallel", "arbitrary")),
        name="mix_merge",
    )(*o_groups, *lse_groups, out_b, gates, gates, wa, wb, wo, h, mod4)


def _rope_tables(s):
    rows = s // GRID_W
    row = jnp.repeat(jnp.arange(rows), GRID_W).astype(F32)
    col = jnp.tile(jnp.arange(GRID_W), rows).astype(F32)
    half = HEAD_DIM // 2
    inv_freq = ROPE_THETA ** (-jnp.arange(0, half, 2, dtype=F32) / half)
    ang = jnp.concatenate([row[:, None] * inv_freq, col[:, None] * inv_freq], axis=-1)
    cos, sin = jnp.cos(ang), jnp.sin(ang)
    return jnp.concatenate([cos, cos], axis=-1), jnp.concatenate([-sin, sin], axis=-1)


def kernel(x, c, w_ada, b_ada, norm_ffn1, w1_ffn1, w3_ffn1, w2_ffn1, norm_mix, w_in, q_norm_a, k_norm_a, q_norm_b, k_norm_b, w_branch_a, w_branch_b, w_out, norm_ffn2, w1_ffn2, w3_ffn2, w2_ffn2, norm_final):
    n_b, s, d = x.shape
    assert w_ada.shape[0] == 1, "the fused pipeline covers a single layer"
    cos_f, sin_f = _rope_tables(s)
    h = x
    for l in range(1):
        mod4 = _ada(c, w_ada[l], b_ada[l]).reshape(n_b, N_MOD, 1, d)
        h, u = _ffn(h, mod4, norm_ffn1[l], w1_ffn1[l].astype(BF16), w3_ffn1[l].astype(BF16),
                    w2_ffn1[l].astype(BF16), (0, 1, 2), "mid", norm_mix[l], (3, 4))
        wi = w_in[l]
        o_qk = 3 * D_A
        gain_a = jnp.concatenate([jnp.tile(q_norm_a[l], N_HEADS_A), jnp.tile(k_norm_a[l], N_HEADS_A)])[None]
        gain_b = jnp.concatenate([jnp.tile(q_norm_b[l], N_Q_B), jnp.tile(k_norm_b[l], N_KV_B)])[None]
        qa = _proj(u, wi[:, :D_A].astype(BF16), "norm", 512, gain_a[:, :D_A], scale=SM_SCALE)
        ka = _proj(u, wi[:, D_A:2 * D_A].astype(BF16), "norm", 512, gain_a[:, D_A:])
        va = _proj(u, wi[:, 2 * D_A:o_qk].astype(BF16), "plain", 512)
        qb = _proj(u, wi[:, o_qk:o_qk + D_QB].astype(BF16), "norm_rope", 512, gain_b[:, :D_QB],
                   cos_f, sin_f, scale=SM_SCALE * LOG2E)
        kb = _proj(u, wi[:, o_qk + D_QB:o_qk + D_QB + D_KVB].astype(BF16), "norm_rope", 256,
                   gain_b[:, D_QB:], cos_f, sin_f)
        vb = _proj(u, wi[:, o_qk + D_QB + D_KVB:o_qk + D_QB + 2 * D_KVB].astype(BF16), "chunk_t", 256)
        gates = _proj(u, wi[:, o_qk + D_QB + 2 * D_KVB:].astype(BF16), "sigmoid", 512)

        o_groups, lse_groups = [], []
        for g, (_, dil) in enumerate(DIL_GROUPS):
            o_g, lse_g = _dilated_group(qa, ka, va, g, dil)
            o_groups.append(o_g)
            lse_groups.append(lse_g)
        out_b = _gqa(qb, kb, vb)
        h = _merge(o_groups, lse_groups, out_b, gates, w_branch_a[l].astype(BF16),
                   w_branch_b[l].astype(BF16), w_out[l].astype(BF16), h, mod4, 5)
        h = _ffn(h, mod4, norm_ffn2[l], w1_ffn2[l].astype(BF16), w3_ffn2[l].astype(BF16),
                 w2_ffn2[l].astype(BF16), (6, 7, 8), "final", norm_final)
    return h
```

```python
import functools

import jax
import jax.numpy as jnp
from jax import lax
from jax.experimental import pallas as pl
from jax.experimental.pallas import tpu as pltpu

F32 = jnp.float32
BF16 = jnp.bfloat16

D_MODEL = 2048
D_FF = 5632
HEAD_DIM = 128
DIL_GROUPS = ((128, 1), (512, 4), (2048, 16))
N_DIL_GROUPS = 3
HEADS_PER_DIL = 4
N_HEADS_A = N_DIL_GROUPS * HEADS_PER_DIL
N_Q_B = 8
N_KV_B = 2
Q_PER_KV = N_Q_B // N_KV_B
GRID_W = 64
ROPE_THETA = 10000.0
ALIBI_MAX = 8.0
N_MOD = 9
EPS = 1e-6
D_A = N_HEADS_A * HEAD_DIM
D_A_OUT = HEADS_PER_DIL * HEAD_DIM
D_QB = N_Q_B * HEAD_DIM
D_KVB = N_KV_B * HEAD_DIM
SM_SCALE = HEAD_DIM ** -0.5
MASK_VALUE = -1e30

LANES = 128
VMEM_LIMIT = 56 * 1024 * 1024


def _params(sem):
    return pltpu.CompilerParams(dimension_semantics=sem, vmem_limit_bytes=VMEM_LIMIT)


def _silu(x):
    return x * jax.nn.sigmoid(x)


def _rms(x, gain):
    ms = jnp.mean(x * x, axis=-1, keepdims=True)
    return x * lax.rsqrt(ms + EPS) * gain


def _rms_head(x, gain):
    sq = x * x
    hi = sq.astype(BF16)
    lo = (sq - hi.astype(F32)).astype(BF16)
    ones = jnp.ones((HEAD_DIM, HEAD_DIM), BF16)
    ssq = (jnp.dot(hi, ones, preferred_element_type=F32) + jnp.dot(lo, ones, preferred_element_type=F32))
    return x * lax.rsqrt(ssq * (1.0 / HEAD_DIM) + EPS) * gain


ADA_TN = 1024


def _ada_kernel(c_ref, w_ref, b_ref, o_ref):
    n_b = c_ref.shape[0]
    for b in range(n_b):
        cs = _silu(c_ref[b])
        for j in range(ADA_TN // LANES):
            sl = slice(j * LANES, (j + 1) * LANES)
            acc = jnp.sum(w_ref[:, sl] * cs, axis=0, keepdims=True)
            o_ref[b, :, sl] = acc + b_ref[:, sl]


def _ada(c, w, b):
    n_b, d = c.shape
    n = w.shape[1]
    c_b = jnp.broadcast_to(c[:, :, None], (n_b, d, LANES))
    return pl.pallas_call(
        _ada_kernel,
        grid=(n // ADA_TN,),
        in_specs=[
            pl.BlockSpec((n_b, d, LANES), lambda j: (0, 0, 0)),
            pl.BlockSpec((d, ADA_TN), lambda j: (0, j)),
            pl.BlockSpec((1, ADA_TN), lambda j: (0, j)),
        ],
        out_specs=pl.BlockSpec((n_b, 1, ADA_TN), lambda j: (0, 0, j)),
        out_shape=jax.ShapeDtypeStruct((n_b, 1, n), F32),
        compiler_params=_params(("arbitrary",)),
        name="ada_mod",
    )(c_b, w, b.reshape(1, n))


FFN_TM = 512
FFN_TF = 512
FFN_ROWS = 256


def _ffn_kernel(x_ref, gain_ref, sh_ref, sc_ref, g_ref, w1_ref, w3_ref, w2_ref, *rest, mode):
    if mode == "mid":
        gain2_ref, sh2_ref, sc2_ref, h_ref, u2_ref, u_scr, acc_scr = rest
    else:
        gainf_ref, y_ref, u_scr, acc_scr = rest
    f = pl.program_id(2)
    nf = pl.num_programs(2)
    tm = x_ref.shape[1]
    row_chunks = [slice(r, r + FFN_ROWS) for r in range(0, tm, FFN_ROWS)]

    def swiglu_part(u):
        a = jnp.dot(u, w1_ref[...], preferred_element_type=F32)
        b = jnp.dot(u, w3_ref[...], preferred_element_type=F32)
        hm = (_silu(a) * b).astype(BF16)
        return jnp.dot(hm, w2_ref[...], preferred_element_type=F32)

    @pl.when(f == 0)
    def _():
        for rows in row_chunks:
            y = _rms(x_ref[0, rows], gain_ref[...])
            u = (y * (1.0 + sc_ref[0, 0]) + sh_ref[0, 0]).astype(BF16)
            u_scr[rows] = u
            acc_scr[rows] = swiglu_part(u)

    @pl.when((f > 0) & (f < nf - 1))
    def _():
        acc_scr[...] += swiglu_part(u_scr[...])

    @pl.when(f == nf - 1)
    def _():
        for rows in row_chunks:
            total = acc_scr[rows] + swiglu_part(u_scr[rows])
            h = x_ref[0, rows] + (0.5 * g_ref[0, 0]) * total
            if mode == "mid":
                h_ref[0, rows] = h
                y2 = _rms(h, gain2_ref[...])
                u2_ref[0, rows] = (y2 * (1.0 + sc2_ref[0, 0]) + sh2_ref[0, 0]).astype(BF16)
            else:
                y_ref[0, rows] = _rms(h, gainf_ref[...])


def _mod_spec(k):
    return pl.BlockSpec((1, 1, 1, D_MODEL), lambda b, i, f, k=k: (b, k, 0, 0))


def _ffn(x, mod4, gain, w1, w3, w2, ks, mode, gain_next, ks_next=None):
    n_b, s, d = x.shape
    tm, tf = FFN_TM, FFN_TF
    grid = (n_b, s // tm, D_FF // tf)
    row_spec = pl.BlockSpec((1, tm, d), lambda b, i, f: (b, i, 0))
    vec_spec = pl.BlockSpec((1, d), lambda b, i, f: (0, 0))
    in_specs = [
        row_spec, vec_spec, _mod_spec(ks[0]), _mod_spec(ks[1]), _mod_spec(ks[2]),
        pl.BlockSpec((d, tf), lambda b, i, f: (0, f)),
        pl.BlockSpec((d, tf), lambda b, i, f: (0, f)),
        pl.BlockSpec((tf, d), lambda b, i, f: (f, 0)),
        vec_spec,
    ]
    args = [x, gain.reshape(1, d), mod4, mod4, mod4, w1, w3, w2, gain_next.reshape(1, d)]
    if mode == "mid":
        in_specs += [_mod_spec(ks_next[0]), _mod_spec(ks_next[1])]
        args += [mod4, mod4]
        out_specs = [row_spec, row_spec]
        out_shape = [jax.ShapeDtypeStruct((n_b, s, d), F32), jax.ShapeDtypeStruct((n_b, s, d), BF16)]
    else:
        out_specs = row_spec
        out_shape = jax.ShapeDtypeStruct((n_b, s, d), F32)
    return pl.pallas_call(
        functools.partial(_ffn_kernel, mode=mode),
        grid=grid,
        in_specs=in_specs,
        out_specs=out_specs,
        out_shape=out_shape,
        scratch_shapes=[pltpu.VMEM((tm, d), BF16), pltpu.VMEM((tm, d), F32)],
        compiler_params=_params(("parallel", "parallel", "arbitrary")),
        name="ffn_" + mode,
    )(*args)


PROJ_TM = 1024


def _proj_kernel(u_ref, w_ref, *rest, mode, tn, scale):
    if mode == "norm":
        gain_ref, o_ref = rest
    elif mode == "norm_rope":
        gain_ref, cos_ref, sin_ref, o_ref = rest
    else:
        (o_ref,) = rest
    acc = jnp.dot(u_ref[0], w_ref[...], preferred_element_type=F32)
    tm = acc.shape[0]
    if mode == "plain":
        o_ref[0] = acc.astype(BF16)
    elif mode == "chunk_t":
        acc_t = acc.T.astype(BF16)
        for hh in range(tn // HEAD_DIM):
            for cc in range(tm // GQA_TK):
                o_ref[0, hh, cc] = acc_t[hh * HEAD_DIM:(hh + 1) * HEAD_DIM, cc * GQA_TK:(cc + 1) * GQA_TK]
    elif mode == "sigmoid":
        o_ref[0] = jax.nn.sigmoid(acc).astype(BF16)
    else:
        for j in range(tn // HEAD_DIM):
            sl = slice(j * HEAD_DIM, (j + 1) * HEAD_DIM)
            y = _rms_head(acc[:, sl], gain_ref[:, sl])
            if mode == "norm_rope":
                y = y * cos_ref[...] + pltpu.roll(y, HEAD_DIM // 2, axis=1) * sin_ref[...]
            if scale != 1.0:
                y = y * scale
            o_ref[0, :, sl] = y.astype(BF16)


def _proj(u, w, mode, tn, gain_row=None, cos=None, sin=None, scale=1.0):
    n_b, s, d = u.shape
    n = w.shape[1]
    tm = PROJ_TM
    in_specs = [
        pl.BlockSpec((1, tm, d), lambda b, i, j: (b, i, 0)),
        pl.BlockSpec((d, tn), lambda b, i, j: (0, j)),
    ]
    args = [u, w]
    if mode in ("norm", "norm_rope"):
        in_specs.append(pl.BlockSpec((1, tn), lambda b, i, j: (0, j)))
        args.append(gain_row)
    if mode == "norm_rope":
        in_specs += [pl.BlockSpec((tm, HEAD_DIM), lambda b, i, j: (i, 0))] * 2
        args += [cos, sin]
    if mode == "chunk_t":
        assert n == tn
        heads = n // HEAD_DIM
        out_spec = pl.BlockSpec((1, heads, tm // GQA_TK, HEAD_DIM, GQA_TK), lambda b, i, j: (b, 0, i, 0, 0))
        out_shape = jax.ShapeDtypeStruct((n_b, heads, s // GQA_TK, HEAD_DIM, GQA_TK), BF16)
    else:
        out_spec = pl.BlockSpec((1, tm, tn), lambda b, i, j: (b, i, j))
        out_shape = jax.ShapeDtypeStruct((n_b, s, n), BF16)
    return pl.pallas_call(
        functools.partial(_proj_kernel, mode=mode, tn=tn, scale=scale),
        grid=(n_b, s // tm, n // tn),
        in_specs=in_specs,
        out_specs=out_spec,
        out_shape=out_shape,
        compiler_params=_params(("parallel", "parallel", "arbitrary")),
        name="proj_" + mode,
    )(*args)


def _proj_classes_kernel(u_ref, w_ref, *rest, norm, scale):
    if norm:
        gain_ref, rest = rest[0], rest[1:]
    o_refs, scr = rest[:N_DIL_GROUPS], rest[N_DIL_GROUPS]
    acc = jnp.dot(u_ref[0], w_ref[...], preferred_element_type=F32)
    tm = acc.shape[0]
    for h in range(HEADS_PER_DIL):
        y = acc[:, h * HEAD_DIM:(h + 1) * HEAD_DIM]
        if norm:
            y = _rms_head(y, gain_ref[...])
        if scale != 1.0:
            y = y * scale
        scr[h] = y
    group = pl.program_id(2)
    for g, (_, dil) in enumerate(DIL_GROUPS):
        @pl.when(group == g)
        def _(g=g, dil=dil):
            for r in range(dil):
                for h in range(HEADS_PER_DIL):
                    o_refs[g][0, r, :, h * HEAD_DIM:(h + 1) * HEAD_DIM] = (
                        scr[h, pl.ds(r, tm // dil, stride=dil), :].astype(BF16))


def _proj_classes(u, w, gain=None, scale=1.0):
    n_b, s, d = u.shape
    tm = PROJ_TM
    norm = gain is not None
    in_specs = [
        pl.BlockSpec((1, tm, d), lambda b, i, g: (b, i, 0)),
        pl.BlockSpec((d, D_A_OUT), lambda b, i, g: (0, g)),
    ]
    args = [u, w]
    if norm:
        in_specs.append(pl.BlockSpec((1, HEAD_DIM), lambda b, i, g: (0, 0)))
        args.append(gain.reshape(1, HEAD_DIM))
    return pl.pallas_call(
        functools.partial(_proj_classes_kernel, norm=norm, scale=scale),
        grid=(n_b, s // tm, N_DIL_GROUPS),
        in_specs=in_specs,
        out_specs=[pl.BlockSpec((1, dil, tm // dil, D_A_OUT), lambda b, i, g: (b, 0, i, 0))
                   for _, dil in DIL_GROUPS],
        out_shape=[jax.ShapeDtypeStruct((n_b, dil, s // dil, D_A_OUT), BF16) for _, dil in DIL_GROUPS],
        scratch_shapes=[pltpu.VMEM((HEADS_PER_DIL, tm, HEAD_DIM), F32)],
        compiler_params=_params(("parallel", "parallel", "arbitrary")),
        name="proj_classes",
    )(*args)


GQA_TQ = 256
GQA_TK = 512
LOG2E = 1.4426950408889634


def _gqa_kernel(q_ref, k_ref, vt_ref, o_ref, acc_ref, s_ref):
    n_chunks = vt_ref.shape[2]
    tq = q_ref.shape[1]
    heads = range(Q_PER_KV)
    nt_dims = (((1,), (1,)), ((), ()))

    def k_chunk(c):
        return k_ref[0, pl.ds(pl.multiple_of(c * GQA_TK, GQA_TK), GQA_TK), :]

    def q_head(g):
        return q_ref[0, :, g * HEAD_DIM:(g + 1) * HEAD_DIM]

    def step(c, c_next, slot, ms, ls):
        k_next = k_chunk(c_next)
        vt = vt_ref[0, 0, c]
        new_ms, new_ls = [], []
        for g in heads:
            s_ref[1 - slot, g] = lax.dot_general(k_next, q_head(g), nt_dims, preferred_element_type=F32)
            m_new = jnp.maximum(ms[g], jnp.max(s_ref[slot, g], axis=0, keepdims=True))
            alpha = jnp.exp2(ms[g] - m_new)
            p = jnp.exp2(s_ref[slot, g] - m_new)
            new_ls.append(alpha * ls[g] + jnp.sum(p, axis=0, keepdims=True))
            new_ms.append(m_new)
            acc_ref[g] = alpha * acc_ref[g] + jnp.dot(vt, p.astype(BF16), preferred_element_type=F32)
        return tuple(new_ms), tuple(new_ls)

    acc_ref[...] = jnp.zeros(acc_ref.shape, F32)
    k0 = k_chunk(0)
    for g in heads:
        s_ref[0, g] = lax.dot_general(k0, q_head(g), nt_dims, preferred_element_type=F32)

    def body(i, carry):
        ms, ls = carry
        ms, ls = step(2 * i, 2 * i + 1, 0, ms, ls)
        return step(2 * i + 1, jnp.minimum(2 * i + 2, n_chunks - 1), 1, ms, ls)

    init = (tuple(jnp.full((1, tq), MASK_VALUE, F32) for _ in heads),
            tuple(jnp.zeros((1, tq), F32) for _ in heads))
    _, ls = lax.fori_loop(0, n_chunks // 2, body, init)
    for g in heads:
        o_ref[0, :, g * HEAD_DIM:(g + 1) * HEAD_DIM] = (acc_ref[g] / ls[g]).T.astype(BF16)


def _gqa(qb, kb, vbt):
    n_b, s, _ = qb.shape
    tq = GQA_TQ
    w = Q_PER_KV * HEAD_DIM
    return pl.pallas_call(
        _gqa_kernel,
        grid=(n_b, N_KV_B, s // tq),
        in_specs=[
            pl.BlockSpec((1, tq, w), lambda b, h, i: (b, i, h)),
            pl.BlockSpec((1, s, HEAD_DIM), lambda b, h, i: (b, 0, h)),
            pl.BlockSpec((1, 1, s // GQA_TK, HEAD_DIM, GQA_TK), lambda b, h, i: (b, h, 0, 0, 0)),
        ],
        out_specs=pl.BlockSpec((1, tq, w), lambda b, h, i: (b, i, h)),
        out_shape=jax.ShapeDtypeStruct((n_b, s, D_QB), BF16),
        scratch_shapes=[pltpu.VMEM((Q_PER_KV, HEAD_DIM, tq), F32),
                        pltpu.VMEM((2, Q_PER_KV, GQA_TK, tq), F32)],
        compiler_params=_params(("parallel", "parallel", "arbitrary")),
        name="gqa_attn",
    )(qb, kb, vbt)


DIL_TQ = 512
DIL_SUB = 128
DIL_HALF = 64


def _dil_kernel(q_ref, kp_ref, kc_ref, kn_ref, vp_ref, vc_ref, vn_ref, o_ref, lse_ref, *, group, dil, n_rows):
    i = pl.program_id(2)
    win = DIL_SUB + 2 * DIL_HALF
    row = lax.broadcasted_iota(jnp.int32, (DIL_SUB, win), 0)
    col = lax.broadcasted_iota(jnp.int32, (DIL_SUB, win), 1)
    dist = jnp.abs(col - DIL_HALF - row)
    in_band = dist <= DIL_HALF
    dist_f = dist.astype(F32) * float(dil)
    for h in range(HEADS_PER_DIL):
        sl = slice(h * HEAD_DIM, (h + 1) * HEAD_DIM)
        slope = 2.0 ** (-ALIBI_MAX * (group * HEADS_PER_DIL + h + 1) / N_HEADS_A)
        k = jnp.concatenate([kp_ref[:, sl], kc_ref[:, sl], kn_ref[:, sl]], axis=0)
        v = jnp.concatenate([vp_ref[:, sl], vc_ref[:, sl], vn_ref[:, sl]], axis=0)
        for j in range(DIL_TQ // DIL_SUB):
            rows = slice(j * DIL_SUB, (j + 1) * DIL_SUB)
            k_abs = i * DIL_TQ + (j * DIL_SUB - DIL_HALF) + col
            valid = in_band & (k_abs >= 0) & (k_abs < n_rows)
            s = lax.dot_general(q_ref[rows, sl], k[j * DIL_SUB:j * DIL_SUB + win],
                                (((1,), (1,)), ((), ())), preferred_element_type=F32)
            s = jnp.where(valid, s - slope * dist_f, MASK_VALUE)
            m = jnp.max(s, axis=-1, keepdims=True)
            p = jnp.exp(s - m)
            l = jnp.sum(p, axis=-1, keepdims=True)
            o = jnp.dot(p.astype(BF16), v[j * DIL_SUB:j * DIL_SUB + win], preferred_element_type=F32)
            o_ref[rows, sl] = o / l
            lse_ref[rows, sl] = jnp.broadcast_to(m + jnp.log(l), (DIL_SUB, HEAD_DIM))


def _dilated_group(q, k, v, group, dil):
    n_b, _, n_rows, w = q.shape
    tq = DIL_TQ
    halo_per_blk = tq // DIL_HALF
    n_halo = n_rows // DIL_HALF
    cur = pl.BlockSpec((None, None, tq, w), lambda b, r, i: (b, r, i, 0))
    prev = pl.BlockSpec((None, None, DIL_HALF, w),
                        lambda b, r, i: (b, r, jnp.maximum(i * halo_per_blk - 1, 0), 0))
    nxt = pl.BlockSpec((None, None, DIL_HALF, w),
                       lambda b, r, i: (b, r, jnp.minimum((i + 1) * halo_per_blk, n_halo - 1), 0))
    return pl.pallas_call(
        functools.partial(_dil_kernel, group=group, dil=dil, n_rows=n_rows),
        grid=(n_b, dil, n_rows // tq),
        in_specs=[cur, prev, cur, nxt, prev, cur, nxt],
        out_specs=[cur, cur],
        out_shape=[jax.ShapeDtypeStruct((n_b, dil, n_rows, w), F32)] * 2,
        compiler_params=_params(("parallel", "parallel", "arbitrary")),
        name="dilated_attn_g%d" % group,
    )(q, k, k, k, v, v, v)


MERGE_TM = 256


def _merge_kernel(o0_ref, o1_ref, o2_ref, l0_ref, l1_ref, l2_ref, ob_ref, sa_ref, sb_ref,
                  wa_ref, wb_ref, wo_ref, h_ref, g_ref, out_ref, nat_ref):
    tm = out_ref.shape[1]

    def natural(ref, dil, slot):
        if dil == 1:
            return ref[0]
        for r in range(dil):
            for h in range(HEADS_PER_DIL):
                nat_ref[slot, h, pl.ds(r, tm // dil, stride=dil), :] = ref[r, :, h * HEAD_DIM:(h + 1) * HEAD_DIM]
        return jnp.concatenate([nat_ref[slot, h] for h in range(HEADS_PER_DIL)], axis=1)

    dils = [dil for _, dil in DIL_GROUPS]
    l0, l1, l2 = [natural(ref, dil, n) for n, (ref, dil) in enumerate(zip((l0_ref, l1_ref, l2_ref), dils))]
    o0, o1, o2 = [natural(ref, dil, N_DIL_GROUPS + n)
                  for n, (ref, dil) in enumerate(zip((o0_ref, o1_ref, o2_ref), dils))]
    mx = jnp.maximum(jnp.maximum(l0, l1), l2)
    e0, e1, e2 = jnp.exp(l0 - mx), jnp.exp(l1 - mx), jnp.exp(l2 - mx)
    oa = (e0 * o0 + e1 * o1 + e2 * o2) / (e0 + e1 + e2)
    pa = jnp.dot(oa.astype(BF16), wa_ref[...], preferred_element_type=F32)
    pb = jnp.dot(ob_ref[0], wb_ref[...], preferred_element_type=F32)
    merged = sa_ref[0].astype(F32) * pa + sb_ref[0].astype(F32) * pb
    mixed = jnp.dot(merged.astype(BF16), wo_ref[...], preferred_element_type=F32)
    out_ref[0] = h_ref[0] + g_ref[0, 0] * mixed


def _merge(o_groups, lse_groups, out_b, gates, wa, wb, wo, h, mod4, k_gate):
    n_b, s, d = h.shape
    tm = MERGE_TM
    a_specs = [pl.BlockSpec((None, dil, tm // dil, D_A_OUT), lambda b, i: (b, 0, i, 0))
               for _, dil in DIL_GROUPS]
    row_spec = pl.BlockSpec((1, tm, d), lambda b, i: (b, i, 0))
    full = lambda shape: pl.BlockSpec(shape, lambda b, i: (0, 0))
    return pl.pallas_call(
        _merge_kernel,
        grid=(n_b, s // tm),
        in_specs=a_specs + a_specs + [
            pl.BlockSpec((1, tm, D_QB), lambda b, i: (b, i, 0)),
            pl.BlockSpec((1, tm, d), lambda b, i: (b, i, 0)),
            pl.BlockSpec((1, tm, d), lambda b, i: (b, i, 1)),
            full((D_A_OUT, d)), full((D_QB, d)), full((d, d)),
            row_spec,
            pl.BlockSpec((1, 1, 1, d), lambda b, i: (b, k_gate, 0, 0)),
        ],
        out_specs=row_spec,
        out_shape=jax.ShapeDtypeStruct((n_b, s, d), F32),
        scratch_shapes=[pltpu.VMEM((2 * N_DIL_GROUPS, HEADS_PER_DIL, tm, HEAD_DIM), F32)],
        compiler_params=_params(("parallel", "arbitrary")),
        name="mix_merge",
    )(*o_groups, *lse_groups, out_b, gates, gates, wa, wb, wo, h, mod4)


def _rope_tables(s):
    rows = s // GRID_W
    row = jnp.repeat(jnp.arange(rows), GRID_W).astype(F32)
    col = jnp.tile(jnp.arange(GRID_W), rows).astype(F32)
    half = HEAD_DIM // 2
    inv_freq = ROPE_THETA ** (-jnp.arange(0, half, 2, dtype=F32) / half)
    ang = jnp.concatenate([row[:, None] * inv_freq, col[:, None] * inv_freq], axis=-1)
    cos, sin = jnp.cos(ang), jnp.sin(ang)
    return jnp.concatenate([cos, cos], axis=-1), jnp.concatenate([-sin, sin], axis=-1)


def kernel(x, c, w_ada, b_ada, norm_ffn1, w1_ffn1, w3_ffn1, w2_ffn1, norm_mix, w_in, q_norm_a, k_norm_a, q_norm_b, k_norm_b, w_branch_a, w_branch_b, w_out, norm_ffn2, w1_ffn2, w3_ffn2, w2_ffn2, norm_final):
    n_b, s, d = x.shape
    assert w_ada.shape[0] == 1, "the fused pipeline covers a single layer"
    cos_f, sin_f = _rope_tables(s)
    h = x
    for l in range(1):
        mod4 = _ada(c, w_ada[l], b_ada[l]).reshape(n_b, N_MOD, 1, d)
        h, u = _ffn(h, mod4, norm_ffn1[l], w1_ffn1[l].astype(BF16), w3_ffn1[l].astype(BF16),
                    w2_ffn1[l].astype(BF16), (0, 1, 2), "mid", norm_mix[l], (3, 4))
        wi = w_in[l]
        o_qk = 3 * D_A
        gain_b = jnp.concatenate([jnp.tile(q_norm_b[l], N_Q_B), jnp.tile(k_norm_b[l], N_KV_B)])[None]
        qa = _proj_classes(u, wi[:, :D_A].astype(BF16), q_norm_a[l], scale=SM_SCALE)
        ka = _proj_classes(u, wi[:, D_A:2 * D_A].astype(BF16), k_norm_a[l])
        va = _proj_classes(u, wi[:, 2 * D_A:o_qk].astype(BF16))
        qb = _proj(u, wi[:, o_qk:o_qk + D_QB].astype(BF16), "norm_rope", 512, gain_b[:, :D_QB],
                   cos_f, sin_f, scale=SM_SCALE * LOG2E)
        kb = _proj(u, wi[:, o_qk + D_QB:o_qk + D_QB + D_KVB].astype(BF16), "norm_rope", 256,
                   gain_b[:, D_QB:], cos_f, sin_f)
        vb = _proj(u, wi[:, o_qk + D_QB + D_KVB:o_qk + D_QB + 2 * D_KVB].astype(BF16), "chunk_t", 256)
        gates = _proj(u, wi[:, o_qk + D_QB + 2 * D_KVB:].astype(BF16), "sigmoid", 512)

        o_groups, lse_groups = [], []
        for g, (_, dil) in enumerate(DIL_GROUPS):
            o_g, lse_g = _dilated_group(qa[g], ka[g], va[g], g, dil)
            o_groups.append(o_g)
            lse_groups.append(lse_g)
        out_b = _gqa(qb, kb, vb)
        h = _merge(o_groups, lse_groups, out_b, gates, w_branch_a[l].astype(BF16),
                   w_branch_b[l].astype(BF16), w_out[l].astype(BF16), h, mod4, 5)
        h = _ffn(h, mod4, norm_ffn2[l], w1_ffn2[l].astype(BF16), w3_ffn2[l].astype(BF16),
                 w2_ffn2[l].astype(BF16), (6, 7, 8), "final", norm_final)
    return h
```

```python
import functools

import jax
import jax.numpy as jnp
from jax import lax
from jax.experimental import pallas as pl
from jax.experimental.pallas import tpu as pltpu

F32 = jnp.float32
BF16 = jnp.bfloat16

D_MODEL = 2048
D_FF = 5632
HEAD_DIM = 128
DIL_GROUPS = ((128, 1), (512, 4), (2048, 16))
N_DIL_GROUPS = 3
HEADS_PER_DIL = 4
N_HEADS_A = N_DIL_GROUPS * HEADS_PER_DIL
N_Q_B = 8
N_KV_B = 2
Q_PER_KV = N_Q_B // N_KV_B
GRID_W = 64
ROPE_THETA = 10000.0
ALIBI_MAX = 8.0
N_MOD = 9
EPS = 1e-6
D_A = N_HEADS_A * HEAD_DIM
D_A_OUT = HEADS_PER_DIL * HEAD_DIM
D_QB = N_Q_B * HEAD_DIM
D_KVB = N_KV_B * HEAD_DIM
SM_SCALE = HEAD_DIM ** -0.5
MASK_VALUE = -1e30

LANES = 128
VMEM_LIMIT = 56 * 1024 * 1024


def _params(sem):
    return pltpu.CompilerParams(dimension_semantics=sem, vmem_limit_bytes=VMEM_LIMIT)


def _silu(x):
    return x * jax.nn.sigmoid(x)


def _rms(x, gain):
    ms = jnp.mean(x * x, axis=-1, keepdims=True)
    return x * lax.rsqrt(ms + EPS) * gain


def _rms_head(x, gain):
    sq = x * x
    hi = sq.astype(BF16)
    lo = (sq - hi.astype(F32)).astype(BF16)
    ones = jnp.ones((HEAD_DIM, HEAD_DIM), BF16)
    ssq = (jnp.dot(hi, ones, preferred_element_type=F32) + jnp.dot(lo, ones, preferred_element_type=F32))
    return x * lax.rsqrt(ssq * (1.0 / HEAD_DIM) + EPS) * gain


ADA_TN = 1024


def _ada_kernel(c_ref, w_ref, b_ref, o_ref):
    n_b = c_ref.shape[0]
    for b in range(n_b):
        cs = _silu(c_ref[b])
        for j in range(ADA_TN // LANES):
            sl = slice(j * LANES, (j + 1) * LANES)
            acc = jnp.sum(w_ref[:, sl] * cs, axis=0, keepdims=True)
            o_ref[b, :, sl] = acc + b_ref[:, sl]


def _ada(c, w, b):
    n_b, d = c.shape
    n = w.shape[1]
    c_b = jnp.broadcast_to(c[:, :, None], (n_b, d, LANES))
    return pl.pallas_call(
        _ada_kernel,
        grid=(n // ADA_TN,),
        in_specs=[
            pl.BlockSpec((n_b, d, LANES), lambda j: (0, 0, 0)),
            pl.BlockSpec((d, ADA_TN), lambda j: (0, j)),
            pl.BlockSpec((1, ADA_TN), lambda j: (0, j)),
        ],
        out_specs=pl.BlockSpec((n_b, 1, ADA_TN), lambda j: (0, 0, j)),
        out_shape=jax.ShapeDtypeStruct((n_b, 1, n), F32),
        compiler_params=_params(("arbitrary",)),
        name="ada_mod",
    )(c_b, w, b.reshape(1, n))


FFN_TM = 512
FFN_TF = 512
FFN_ROWS = 256


def _ffn_kernel(x_ref, gain_ref, sh_ref, sc_ref, g_ref, w1_ref, w3_ref, w2_ref, *rest, mode):
    if mode == "mid":
        gain2_ref, sh2_ref, sc2_ref, h_ref, u2_ref, u_scr, acc_scr = rest
    else:
        gainf_ref, y_ref, u_scr, acc_scr = rest
    f = pl.program_id(2)
    nf = pl.num_programs(2)
    tm = x_ref.shape[1]
    row_chunks = [slice(r, r + FFN_ROWS) for r in range(0, tm, FFN_ROWS)]

    def swiglu_part(u):
        a = jnp.dot(u, w1_ref[...], preferred_element_type=F32)
        b = jnp.dot(u, w3_ref[...], preferred_element_type=F32)
        hm = (_silu(a) * b).astype(BF16)
        return jnp.dot(hm, w2_ref[...], preferred_element_type=F32)

    @pl.when(f == 0)
    def _():
        for rows in row_chunks:
            y = _rms(x_ref[0, rows], gain_ref[...])
            u = (y * (1.0 + sc_ref[0, 0]) + sh_ref[0, 0]).astype(BF16)
            u_scr[rows] = u
            acc_scr[rows] = swiglu_part(u)

    @pl.when((f > 0) & (f < nf - 1))
    def _():
        acc_scr[...] += swiglu_part(u_scr[...])

    @pl.when(f == nf - 1)
    def _():
        for rows in row_chunks:
            total = acc_scr[rows] + swiglu_part(u_scr[rows])
            h = x_ref[0, rows] + (0.5 * g_ref[0, 0]) * total
            if mode == "mid":
                h_ref[0, rows] = h
                y2 = _rms(h, gain2_ref[...])
                u2_ref[0, rows] = (y2 * (1.0 + sc2_ref[0, 0]) + sh2_ref[0, 0]).astype(BF16)
            else:
                y_ref[0, rows] = _rms(h, gainf_ref[...])


def _mod_spec(k):
    return pl.BlockSpec((1, 1, 1, D_MODEL), lambda b, i, f, k=k: (b, k, 0, 0))


def _ffn(x, mod4, gain, w1, w3, w2, ks, mode, gain_next, ks_next=None):
    n_b, s, d = x.shape
    tm, tf = FFN_TM, FFN_TF
    grid = (n_b, s // tm, D_FF // tf)
    row_spec = pl.BlockSpec((1, tm, d), lambda b, i, f: (b, i, 0))
    vec_spec = pl.BlockSpec((1, d), lambda b, i, f: (0, 0))
    in_specs = [
        row_spec, vec_spec, _mod_spec(ks[0]), _mod_spec(ks[1]), _mod_spec(ks[2]),
        pl.BlockSpec((d, tf), lambda b, i, f: (0, f)),
        pl.BlockSpec((d, tf), lambda b, i, f: (0, f)),
        pl.BlockSpec((tf, d), lambda b, i, f: (f, 0)),
        vec_spec,
    ]
    args = [x, gain.reshape(1, d), mod4, mod4, mod4, w1, w3, w2, gain_next.reshape(1, d)]
    if mode == "mid":
        in_specs += [_mod_spec(ks_next[0]), _mod_spec(ks_next[1])]
        args += [mod4, mod4]
        out_specs = [row_spec, row_spec]
        out_shape = [jax.ShapeDtypeStruct((n_b, s, d), F32), jax.ShapeDtypeStruct((n_b, s, d), BF16)]
    else:
        out_specs = row_spec
        out_shape = jax.ShapeDtypeStruct((n_b, s, d), F32)
    return pl.pallas_call(
        functools.partial(_ffn_kernel, mode=mode),
        grid=grid,
        in_specs=in_specs,
        out_specs=out_specs,
        out_shape=out_shape,
        scratch_shapes=[pltpu.VMEM((tm, d), BF16), pltpu.VMEM((tm, d), F32)],
        compiler_params=_params(("parallel", "parallel", "arbitrary")),
        name="ffn_" + mode,
    )(*args)


PROJ_TM = 1024


def _proj_kernel(u_ref, w_ref, *rest, mode, tn, scale):
    if mode == "norm":
        gain_ref, o_ref = rest
    elif mode == "norm_rope":
        gain_ref, cos_ref, sin_ref, o_ref = rest
    else:
        (o_ref,) = rest
    acc = jnp.dot(u_ref[0], w_ref[...], preferred_element_type=F32)
    tm = acc.shape[0]
    if mode == "plain":
        o_ref[0] = acc.astype(BF16)
    elif mode == "chunk_t":
        acc_t = acc.T.astype(BF16)
        for hh in range(tn // HEAD_DIM):
            for cc in range(tm // GQA_TK):
                o_ref[0, hh, cc] = acc_t[hh * HEAD_DIM:(hh + 1) * HEAD_DIM, cc * GQA_TK:(cc + 1) * GQA_TK]
    elif mode == "sigmoid":
        o_ref[0] = jax.nn.sigmoid(acc).astype(BF16)
    else:
        for j in range(tn // HEAD_DIM):
            sl = slice(j * HEAD_DIM, (j + 1) * HEAD_DIM)
            y = _rms_head(acc[:, sl], gain_ref[:, sl])
            if mode == "norm_rope":
                y = y * cos_ref[...] + pltpu.roll(y, HEAD_DIM // 2, axis=1) * sin_ref[...]
            if scale != 1.0:
                y = y * scale
            o_ref[0, :, sl] = y.astype(BF16)


def _proj(u, w, mode, tn, gain_row=None, cos=None, sin=None, scale=1.0):
    n_b, s, d = u.shape
    n = w.shape[1]
    tm = PROJ_TM
    in_specs = [
        pl.BlockSpec((1, tm, d), lambda b, i, j: (b, i, 0)),
        pl.BlockSpec((d, tn), lambda b, i, j: (0, j)),
    ]
    args = [u, w]
    if mode in ("norm", "norm_rope"):
        in_specs.append(pl.BlockSpec((1, tn), lambda b, i, j: (0, j)))
        args.append(gain_row)
    if mode == "norm_rope":
        in_specs += [pl.BlockSpec((tm, HEAD_DIM), lambda b, i, j: (i, 0))] * 2
        args += [cos, sin]
    if mode == "chunk_t":
        assert n == tn
        heads = n // HEAD_DIM
        out_spec = pl.BlockSpec((1, heads, tm // GQA_TK, HEAD_DIM, GQA_TK), lambda b, i, j: (b, 0, i, 0, 0))
        out_shape = jax.ShapeDtypeStruct((n_b, heads, s // GQA_TK, HEAD_DIM, GQA_TK), BF16)
    else:
        out_spec = pl.BlockSpec((1, tm, tn), lambda b, i, j: (b, i, j))
        out_shape = jax.ShapeDtypeStruct((n_b, s, n), BF16)
    return pl.pallas_call(
        functools.partial(_proj_kernel, mode=mode, tn=tn, scale=scale),
        grid=(n_b, s // tm, n // tn),
        in_specs=in_specs,
        out_specs=out_spec,
        out_shape=out_shape,
        compiler_params=_params(("parallel", "parallel", "arbitrary")),
        name="proj_" + mode,
    )(*args)


def _proj_classes_kernel(u_ref, w_ref, *rest, norm, scale):
    if norm:
        gain_ref, rest = rest[0], rest[1:]
    o_refs, scr = rest[:N_DIL_GROUPS], rest[N_DIL_GROUPS]
    acc = jnp.dot(u_ref[0], w_ref[...], preferred_element_type=F32)
    tm = acc.shape[0]
    for g, (_, dil) in enumerate(DIL_GROUPS):
        for h in range(HEADS_PER_DIL):
            head = g * HEADS_PER_DIL + h
            lanes = slice(h * HEAD_DIM, (h + 1) * HEAD_DIM)
            y = acc[:, head * HEAD_DIM:(head + 1) * HEAD_DIM]
            if norm:
                y = _rms_head(y, gain_ref[...])
            if scale != 1.0:
                y = y * scale
            if dil == 1:
                o_refs[g][0, 0, :, lanes] = y.astype(BF16)
                continue
            scr[head] = y
            for r in range(dil):
                o_refs[g][0, r, :, lanes] = scr[head, pl.ds(r, tm // dil, stride=dil), :].astype(BF16)


def _proj_classes(u, w, gain=None, scale=1.0):
    n_b, s, d = u.shape
    tm = PROJ_TM
    norm = gain is not None
    in_specs = [
        pl.BlockSpec((1, tm, d), lambda b, i: (b, i, 0)),
        pl.BlockSpec((d, D_A), lambda b, i: (0, 0)),
    ]
    args = [u, w]
    if norm:
        in_specs.append(pl.BlockSpec((1, HEAD_DIM), lambda b, i: (0, 0)))
        args.append(gain.reshape(1, HEAD_DIM))
    return pl.pallas_call(
        functools.partial(_proj_classes_kernel, norm=norm, scale=scale),
        grid=(n_b, s // tm),
        in_specs=in_specs,
        out_specs=[pl.BlockSpec((1, dil, tm // dil, D_A_OUT), lambda b, i: (b, 0, i, 0))
                   for _, dil in DIL_GROUPS],
        out_shape=[jax.ShapeDtypeStruct((n_b, dil, s // dil, D_A_OUT), BF16) for _, dil in DIL_GROUPS],
        scratch_shapes=[pltpu.VMEM((N_HEADS_A, tm, HEAD_DIM), F32)],
        compiler_params=_params(("parallel", "arbitrary")),
        name="proj_classes",
    )(*args)


GQA_TQ = 512
GQA_TK = 512
LOG2E = 1.4426950408889634


def _gqa_kernel(q_ref, k_ref, vt_ref, o_ref, acc_ref, s_ref):
    n_chunks = vt_ref.shape[2]
    tq = q_ref.shape[1]
    heads = range(Q_PER_KV)
    nt_dims = (((1,), (1,)), ((), ()))

    def k_chunk(c):
        return k_ref[0, pl.ds(pl.multiple_of(c * GQA_TK, GQA_TK), GQA_TK), :]

    def q_head(g):
        return q_ref[0, :, g * HEAD_DIM:(g + 1) * HEAD_DIM]

    def step(c, c_next, slot, ms, ls):
        if c_next is not None:
            k_next = k_chunk(c_next)
        vt = vt_ref[0, 0, c]
        new_ms, new_ls = [], []
        for g in heads:
            if c_next is not None:
                s_ref[1 - slot, g] = lax.dot_general(k_next, q_head(g), nt_dims, preferred_element_type=F32)
            m_new = jnp.maximum(ms[g], jnp.max(s_ref[slot, g], axis=0, keepdims=True))
            alpha = jnp.exp2(ms[g] - m_new)
            p = jnp.exp2(s_ref[slot, g] - m_new)
            new_ls.append(alpha * ls[g] + jnp.sum(p, axis=0, keepdims=True))
            new_ms.append(m_new)
            acc_ref[g] = alpha * acc_ref[g] + jnp.dot(vt, p.astype(BF16), preferred_element_type=F32)
        return tuple(new_ms), tuple(new_ls)

    acc_ref[...] = jnp.zeros(acc_ref.shape, F32)
    k0 = k_chunk(0)
    for g in heads:
        s_ref[0, g] = lax.dot_general(k0, q_head(g), nt_dims, preferred_element_type=F32)

    def body(i, carry):
        ms, ls = carry
        ms, ls = step(2 * i, 2 * i + 1, 0, ms, ls)
        return step(2 * i + 1, 2 * i + 2, 1, ms, ls)

    init = (tuple(jnp.full((1, tq), MASK_VALUE, F32) for _ in heads),
            tuple(jnp.zeros((1, tq), F32) for _ in heads))
    ms, ls = lax.fori_loop(0, n_chunks // 2 - 1, body, init)
    ms, ls = step(n_chunks - 2, n_chunks - 1, 0, ms, ls)
    _, ls = step(n_chunks - 1, None, 1, ms, ls)
    for g in heads:
        o_ref[0, :, g * HEAD_DIM:(g + 1) * HEAD_DIM] = (acc_ref[g] / ls[g]).T.astype(BF16)


def _gqa(qb, kb, vbt):
    n_b, s, _ = qb.shape
    tq = GQA_TQ
    w = Q_PER_KV * HEAD_DIM
    return pl.pallas_call(
        _gqa_kernel,
        grid=(n_b, N_KV_B, s // tq),
        in_specs=[
            pl.BlockSpec((1, tq, w), lambda b, h, i: (b, i, h)),
            pl.BlockSpec((1, s, HEAD_DIM), lambda b, h, i: (b, 0, h)),
            pl.BlockSpec((1, 1, s // GQA_TK, HEAD_DIM, GQA_TK), lambda b, h, i: (b, h, 0, 0, 0)),
        ],
        out_specs=pl.BlockSpec((1, tq, w), lambda b, h, i: (b, i, h)),
        out_shape=jax.ShapeDtypeStruct((n_b, s, D_QB), BF16),
        scratch_shapes=[pltpu.VMEM((Q_PER_KV, HEAD_DIM, tq), F32),
                        pltpu.VMEM((2, Q_PER_KV, GQA_TK, tq), F32)],
        compiler_params=_params(("parallel", "parallel", "arbitrary")),
        name="gqa_attn",
    )(qb, kb, vbt)


DIL_TQ = 512
DIL_SUB = 128
DIL_HALF = 64


def _dil_kernel(q_ref, kp_ref, kc_ref, kn_ref, vp_ref, vc_ref, vn_ref, o_ref, lse_ref, *, group, dil, n_rows):
    i = pl.program_id(2)
    win = DIL_SUB + 2 * DIL_HALF
    row = lax.broadcasted_iota(jnp.int32, (DIL_SUB, win), 0)
    col = lax.broadcasted_iota(jnp.int32, (DIL_SUB, win), 1)
    dist = jnp.abs(col - DIL_HALF - row)
    in_band = dist <= DIL_HALF
    dist_f = dist.astype(F32) * float(dil)
    for h in range(HEADS_PER_DIL):
        sl = slice(h * HEAD_DIM, (h + 1) * HEAD_DIM)
        slope = 2.0 ** (-ALIBI_MAX * (group * HEADS_PER_DIL + h + 1) / N_HEADS_A)
        k = jnp.concatenate([kp_ref[:, sl], kc_ref[:, sl], kn_ref[:, sl]], axis=0)
        v = jnp.concatenate([vp_ref[:, sl], vc_ref[:, sl], vn_ref[:, sl]], axis=0)
        for j in range(DIL_TQ // DIL_SUB):
            rows = slice(j * DIL_SUB, (j + 1) * DIL_SUB)
            k_abs = i * DIL_TQ + (j * DIL_SUB - DIL_HALF) + col
            valid = in_band & (k_abs >= 0) & (k_abs < n_rows)
            s = lax.dot_general(q_ref[rows, sl], k[j * DIL_SUB:j * DIL_SUB + win],
                                (((1,), (1,)), ((), ())), preferred_element_type=F32)
            s = jnp.where(valid, s - slope * dist_f, MASK_VALUE)
            m = jnp.max(s, axis=-1, keepdims=True)
            p = jnp.exp(s - m)
            l = jnp.sum(p, axis=-1, keepdims=True)
            o = jnp.dot(p.astype(BF16), v[j * DIL_SUB:j * DIL_SUB + win], preferred_element_type=F32)
            o_ref[rows, sl] = o / l
            lse_ref[rows, sl] = jnp.broadcast_to(m + jnp.log(l), (DIL_SUB, HEAD_DIM))


def _dilated_group(q, k, v, group, dil):
    n_b, _, n_rows, w = q.shape
    tq = DIL_TQ
    halo_per_blk = tq // DIL_HALF
    n_halo = n_rows // DIL_HALF
    cur = pl.BlockSpec((None, None, tq, w), lambda b, r, i: (b, r, i, 0))
    prev = pl.BlockSpec((None, None, DIL_HALF, w),
                        lambda b, r, i: (b, r, jnp.maximum(i * halo_per_blk - 1, 0), 0))
    nxt = pl.BlockSpec((None, None, DIL_HALF, w),
                       lambda b, r, i: (b, r, jnp.minimum((i + 1) * halo_per_blk, n_halo - 1), 0))
    return pl.pallas_call(
        functools.partial(_dil_kernel, group=group, dil=dil, n_rows=n_rows),
        grid=(n_b, dil, n_rows // tq),
        in_specs=[cur, prev, cur, nxt, prev, cur, nxt],
        out_specs=[cur, cur],
        out_shape=[jax.ShapeDtypeStruct((n_b, dil, n_rows, w), F32)] * 2,
        compiler_params=_params(("parallel", "parallel", "arbitrary")),
        name="dilated_attn_g%d" % group,
    )(q, k, k, k, v, v, v)


MERGE_TM = 256


def _merge_kernel(o0_ref, o1_ref, o2_ref, l0_ref, l1_ref, l2_ref, ob_ref, sa_ref, sb_ref,
                  wa_ref, wb_ref, wo_ref, h_ref, g_ref, out_ref, nat_ref):
    tm = out_ref.shape[1]

    def natural(ref, dil, slot):
        if dil == 1:
            return ref[0]
        for r in range(dil):
            for h in range(HEADS_PER_DIL):
                nat_ref[slot, h, pl.ds(r, tm // dil, stride=dil), :] = ref[r, :, h * HEAD_DIM:(h + 1) * HEAD_DIM]
        return jnp.concatenate([nat_ref[slot, h] for h in range(HEADS_PER_DIL)], axis=1)

    dils = [dil for _, dil in DIL_GROUPS]
    l0, l1, l2 = [natural(ref, dil, n) for n, (ref, dil) in enumerate(zip((l0_ref, l1_ref, l2_ref), dils))]
    o0, o1, o2 = [natural(ref, dil, N_DIL_GROUPS + n)
                  for n, (ref, dil) in enumerate(zip((o0_ref, o1_ref, o2_ref), dils))]
    mx = jnp.maximum(jnp.maximum(l0, l1), l2)
    e0, e1, e2 = jnp.exp(l0 - mx), jnp.exp(l1 - mx), jnp.exp(l2 - mx)
    oa = (e0 * o0 + e1 * o1 + e2 * o2) / (e0 + e1 + e2)
    pa = jnp.dot(oa.astype(BF16), wa_ref[...], preferred_element_type=F32)
    pb = jnp.dot(ob_ref[0], wb_ref[...], preferred_element_type=F32)
    merged = sa_ref[0].astype(F32) * pa + sb_ref[0].astype(F32) * pb
    mixed = jnp.dot(merged.astype(BF16), wo_ref[...], preferred_element_type=F32)
    out_ref[0] = h_ref[0] + g_ref[0, 0] * mixed


def _merge(o_groups, lse_groups, out_b, gates, wa, wb, wo, h, mod4, k_gate):
    n_b, s, d = h.shape
    tm = MERGE_TM
    a_specs = [pl.BlockSpec((None, dil, tm // dil, D_A_OUT), lambda b, i: (b, 0, i, 0))
               for _, dil in DIL_GROUPS]
    row_spec = pl.BlockSpec((1, tm, d), lambda b, i: (b, i, 0))
    full = lambda shape: pl.BlockSpec(shape, lambda b, i: (0, 0))
    return pl.pallas_call(
        _merge_kernel,
        grid=(n_b, s // tm),
        in_specs=a_specs + a_specs + [
            pl.BlockSpec((1, tm, D_QB), lambda b, i: (b, i, 0)),
            pl.BlockSpec((1, tm, d), lambda b, i: (b, i, 0)),
            pl.BlockSpec((1, tm, d), lambda b, i: (b, i, 1)),
            full((D_A_OUT, d)), full((D_QB, d)), full((d, d)),
            row_spec,
            pl.BlockSpec((1, 1, 1, d), lambda b, i: (b, k_gate, 0, 0)),
        ],
        out_specs=row_spec,
        out_shape=jax.ShapeDtypeStruct((n_b, s, d), F32),
        scratch_shapes=[pltpu.VMEM((2 * N_DIL_GROUPS, HEADS_PER_DIL, tm, HEAD_DIM), F32)],
        compiler_params=_params(("parallel", "arbitrary")),
        name="mix_merge",
    )(*o_groups, *lse_groups, out_b, gates, gates, wa, wb, wo, h, mod4)


def _rope_tables(s):
    rows = s // GRID_W
    row = jnp.repeat(jnp.arange(rows), GRID_W).astype(F32)
    col = jnp.tile(jnp.arange(GRID_W), rows).astype(F32)
    half = HEAD_DIM // 2
    inv_freq = ROPE_THETA ** (-jnp.arange(0, half, 2, dtype=F32) / half)
    ang = jnp.concatenate([row[:, None] * inv_freq, col[:, None] * inv_freq], axis=-1)
    cos, sin = jnp.cos(ang), jnp.sin(ang)
    return jnp.concatenate([cos, cos], axis=-1), jnp.concatenate([-sin, sin], axis=-1)


def kernel(x, c, w_ada, b_ada, norm_ffn1, w1_ffn1, w3_ffn1, w2_ffn1, norm_mix, w_in, q_norm_a, k_norm_a, q_norm_b, k_norm_b, w_branch_a, w_branch_b, w_out, norm_ffn2, w1_ffn2, w3_ffn2, w2_ffn2, norm_final):
    n_b, s, d = x.shape
    assert w_ada.shape[0] == 1, "the fused pipeline covers a single layer"
    cos_f, sin_f = _rope_tables(s)
    h = x
    for l in range(1):
        mod4 = _ada(c, w_ada[l], b_ada[l]).reshape(n_b, N_MOD, 1, d)
        h, u = _ffn(h, mod4, norm_ffn1[l], w1_ffn1[l].astype(BF16), w3_ffn1[l].astype(BF16),
                    w2_ffn1[l].astype(BF16), (0, 1, 2), "mid", norm_mix[l], (3, 4))
        wi = w_in[l]
        o_qk = 3 * D_A
        gain_b = jnp.concatenate([jnp.tile(q_norm_b[l], N_Q_B), jnp.tile(k_norm_b[l], N_KV_B)])[None]
        qa = _proj_classes(u, wi[:, :D_A].astype(BF16), q_norm_a[l], scale=SM_SCALE)
        ka = _proj_classes(u, wi[:, D_A:2 * D_A].astype(BF16), k_norm_a[l])
        va = _proj_classes(u, wi[:, 2 * D_A:o_qk].astype(BF16))
        qb = _proj(u, wi[:, o_qk:o_qk + D_QB].astype(BF16), "norm_rope", 1024, gain_b[:, :D_QB],
                   cos_f, sin_f, scale=SM_SCALE * LOG2E)
        kb = _proj(u, wi[:, o_qk + D_QB:o_qk + D_QB + D_KVB].astype(BF16), "norm_rope", 256,
                   gain_b[:, D_QB:], cos_f, sin_f)
        vb = _proj(u, wi[:, o_qk + D_QB + D_KVB:o_qk + D_QB + 2 * D_KVB].astype(BF16), "chunk_t", 256)
        gates = _proj(u, wi[:, o_qk + D_QB + 2 * D_KVB:].astype(BF16), "sigmoid", 1024)

        o_groups, lse_groups = [], []
        for g, (_, dil) in enumerate(DIL_GROUPS):
            o_g, lse_g = _dilated_group(qa[g], ka[g], va[g], g, dil)
            o_groups.append(o_g)
            lse_groups.append(lse_g)
        out_b = _gqa(qb, kb, vb)
        h = _merge(o_groups, lse_groups, out_b, gates, w_branch_a[l].astype(BF16),
                   w_branch_b[l].astype(BF16), w_out[l].astype(BF16), h, mod4, 5)
        h = _ffn(h, mod4, norm_ffn2[l], w1_ffn2[l].astype(BF16), w3_ffn2[l].astype(BF16),
                 w2_ffn2[l].astype(BF16), (6, 7, 8), "final", norm_final)
    return h
```

```python
import functools

import jax
import jax.numpy as jnp
from jax import lax
from jax.experimental import pallas as pl
from jax.experimental.pallas import tpu as pltpu

F32 = jnp.float32
BF16 = jnp.bfloat16

D_MODEL = 2048
D_FF = 5632
HEAD_DIM = 128
DIL_GROUPS = ((128, 1), (512, 4), (2048, 16))
N_DIL_GROUPS = 3
HEADS_PER_DIL = 4
N_HEADS_A = N_DIL_GROUPS * HEADS_PER_DIL
N_Q_B = 8
N_KV_B = 2
Q_PER_KV = N_Q_B // N_KV_B
GRID_W = 64
ROPE_THETA = 10000.0
ALIBI_MAX = 8.0
N_MOD = 9
EPS = 1e-6
D_A = N_HEADS_A * HEAD_DIM
D_A_OUT = HEADS_PER_DIL * HEAD_DIM
D_QB = N_Q_B * HEAD_DIM
D_KVB = N_KV_B * HEAD_DIM
SM_SCALE = HEAD_DIM ** -0.5
MASK_VALUE = -1e30

LANES = 128
VMEM_LIMIT = 56 * 1024 * 1024


def _params(sem):
    return pltpu.CompilerParams(dimension_semantics=sem, vmem_limit_bytes=VMEM_LIMIT)


def _silu(x):
    return x * jax.nn.sigmoid(x)


def _rms(x, gain):
    ms = jnp.mean(x * x, axis=-1, keepdims=True)
    return x * lax.rsqrt(ms + EPS) * gain


def _rms_head(x, gain):
    sq = x * x
    hi = sq.astype(BF16)
    lo = (sq - hi.astype(F32)).astype(BF16)
    ones = jnp.ones((HEAD_DIM, HEAD_DIM), BF16)
    ssq = (jnp.dot(hi, ones, preferred_element_type=F32) + jnp.dot(lo, ones, preferred_element_type=F32))
    return x * lax.rsqrt(ssq * (1.0 / HEAD_DIM) + EPS) * gain


ADA_TN = 1024


def _ada_kernel(c_ref, w_ref, b_ref, o_ref):
    n_b = c_ref.shape[0]
    for b in range(n_b):
        cs = _silu(c_ref[b])
        for j in range(ADA_TN // LANES):
            sl = slice(j * LANES, (j + 1) * LANES)
            acc = jnp.sum(w_ref[:, sl] * cs, axis=0, keepdims=True)
            o_ref[b, :, sl] = acc + b_ref[:, sl]


def _ada(c, w, b):
    n_b, d = c.shape
    n = w.shape[1]
    c_b = jnp.broadcast_to(c[:, :, None], (n_b, d, LANES))
    return pl.pallas_call(
        _ada_kernel,
        grid=(n // ADA_TN,),
        in_specs=[
            pl.BlockSpec((n_b, d, LANES), lambda j: (0, 0, 0)),
            pl.BlockSpec((d, ADA_TN), lambda j: (0, j)),
            pl.BlockSpec((1, ADA_TN), lambda j: (0, j)),
        ],
        out_specs=pl.BlockSpec((n_b, 1, ADA_TN), lambda j: (0, 0, j)),
        out_shape=jax.ShapeDtypeStruct((n_b, 1, n), F32),
        compiler_params=_params(("arbitrary",)),
        name="ada_mod",
    )(c_b, w, b.reshape(1, n))


FFN_TM = 1024
FFN_TF = 512
FFN_ROWS = 256
FFN_MID_ROWS = 512


def _ffn_kernel(x_ref, gain_ref, sh_ref, sc_ref, g_ref, w1_ref, w3_ref, w2_ref, *rest, mode):
    if mode == "mid":
        gain2_ref, sh2_ref, sc2_ref, h_ref, u2_ref, u_scr = rest
        acc_ref = h_ref
    else:
        gainf_ref, y_ref, u_scr = rest
        acc_ref = y_ref
    f = pl.program_id(2)
    nf = pl.num_programs(2)
    tm = x_ref.shape[1]
    row_chunks = [slice(r, r + FFN_ROWS) for r in range(0, tm, FFN_ROWS)]
    mid_chunks = [slice(r, r + FFN_MID_ROWS) for r in range(0, tm, FFN_MID_ROWS)]

    def swiglu_part(u):
        a = jnp.dot(u, w1_ref[...], preferred_element_type=F32)
        b = jnp.dot(u, w3_ref[...], preferred_element_type=F32)
        hm = (_silu(a) * b).astype(BF16)
        return jnp.dot(hm, w2_ref[...], preferred_element_type=F32)

    @pl.when(f == 0)
    def _():
        for rows in row_chunks:
            y = _rms(x_ref[0, rows], gain_ref[...])
            u = (y * (1.0 + sc_ref[0, 0]) + sh_ref[0, 0]).astype(BF16)
            u_scr[rows] = u
            acc_ref[0, rows] = swiglu_part(u)

    @pl.when((f > 0) & (f < nf - 1))
    def _():
        for rows in mid_chunks:
            acc_ref[0, rows] += swiglu_part(u_scr[rows])

    @pl.when(f == nf - 1)
    def _():
        for rows in row_chunks:
            total = acc_ref[0, rows] + swiglu_part(u_scr[rows])
            h = x_ref[0, rows] + (0.5 * g_ref[0, 0]) * total
            if mode == "mid":
                h_ref[0, rows] = h
                y2 = _rms(h, gain2_ref[...])
                u2_ref[0, rows] = (y2 * (1.0 + sc2_ref[0, 0]) + sh2_ref[0, 0]).astype(BF16)
            else:
                y_ref[0, rows] = _rms(h, gainf_ref[...])


def _mod_spec(k):
    return pl.BlockSpec((1, 1, 1, D_MODEL), lambda b, i, f, k=k: (b, k, 0, 0))


def _ffn(x, mod4, gain, w1, w3, w2, ks, mode, gain_next, ks_next=None):
    n_b, s, d = x.shape
    tm, tf = FFN_TM, FFN_TF
    grid = (n_b, s // tm, D_FF // tf)
    row_spec = pl.BlockSpec((1, tm, d), lambda b, i, f: (b, i, 0))
    x_spec = pl.BlockSpec((1, tm, d), lambda b, i, f: (b, i, 0), pipeline_mode=pl.Buffered(1))
    vec_spec = pl.BlockSpec((1, d), lambda b, i, f: (0, 0))
    in_specs = [
        x_spec, vec_spec, _mod_spec(ks[0]), _mod_spec(ks[1]), _mod_spec(ks[2]),
        pl.BlockSpec((d, tf), lambda b, i, f: (0, f)),
        pl.BlockSpec((d, tf), lambda b, i, f: (0, f)),
        pl.BlockSpec((tf, d), lambda b, i, f: (f, 0)),
        vec_spec,
    ]
    args = [x, gain.reshape(1, d), mod4, mod4, mod4, w1, w3, w2, gain_next.reshape(1, d)]
    if mode == "mid":
        in_specs += [_mod_spec(ks_next[0]), _mod_spec(ks_next[1])]
        args += [mod4, mod4]
        out_specs = [row_spec, row_spec]
        out_shape = [jax.ShapeDtypeStruct((n_b, s, d), F32), jax.ShapeDtypeStruct((n_b, s, d), BF16)]
    else:
        out_specs = row_spec
        out_shape = jax.ShapeDtypeStruct((n_b, s, d), F32)
    return pl.pallas_call(
        functools.partial(_ffn_kernel, mode=mode),
        grid=grid,
        in_specs=in_specs,
        out_specs=out_specs,
        out_shape=out_shape,
        scratch_shapes=[pltpu.VMEM((tm, d), BF16)],
        compiler_params=_params(("parallel", "parallel", "arbitrary")),
        name="ffn_" + mode,
    )(*args)


PROJ_TM = 1024


def _proj_kernel(u_ref, w_ref, *rest, mode, tn, scale):
    if mode == "norm":
        gain_ref, o_ref = rest
    elif mode == "norm_rope":
        gain_ref, cos_ref, sin_ref, o_ref = rest
    else:
        (o_ref,) = rest
    acc = jnp.dot(u_ref[0], w_ref[...], preferred_element_type=F32)
    tm = acc.shape[0]
    if mode == "plain":
        o_ref[0] = acc.astype(BF16)
    elif mode == "chunk_t":
        acc_t = acc.T.astype(BF16)
        for hh in range(tn // HEAD_DIM):
            for cc in range(tm // GQA_TK):
                o_ref[0, hh, cc] = acc_t[hh * HEAD_DIM:(hh + 1) * HEAD_DIM, cc * GQA_TK:(cc + 1) * GQA_TK]
    elif mode == "sigmoid":
        o_ref[0] = jax.nn.sigmoid(acc).astype(BF16)
    else:
        for j in range(tn // HEAD_DIM):
            sl = slice(j * HEAD_DIM, (j + 1) * HEAD_DIM)
            y = _rms_head(acc[:, sl], gain_ref[:, sl])
            if mode == "norm_rope":
                y = y * cos_ref[...] + pltpu.roll(y, HEAD_DIM // 2, axis=1) * sin_ref[...]
            if scale != 1.0:
                y = y * scale
            o_ref[0, :, sl] = y.astype(BF16)


def _proj(u, w, mode, tn, gain_row=None, cos=None, sin=None, scale=1.0):
    n_b, s, d = u.shape
    n = w.shape[1]
    tm = PROJ_TM
    in_specs = [
        pl.BlockSpec((1, tm, d), lambda b, i, j: (b, i, 0)),
        pl.BlockSpec((d, tn), lambda b, i, j: (0, j)),
    ]
    args = [u, w]
    if mode in ("norm", "norm_rope"):
        in_specs.append(pl.BlockSpec((1, tn), lambda b, i, j: (0, j)))
        args.append(gain_row)
    if mode == "norm_rope":
        in_specs += [pl.BlockSpec((tm, HEAD_DIM), lambda b, i, j: (i, 0))] * 2
        args += [cos, sin]
    if mode == "chunk_t":
        assert n == tn
        heads = n // HEAD_DIM
        out_spec = pl.BlockSpec((1, heads, tm // GQA_TK, HEAD_DIM, GQA_TK), lambda b, i, j: (b, 0, i, 0, 0))
        out_shape = jax.ShapeDtypeStruct((n_b, heads, s // GQA_TK, HEAD_DIM, GQA_TK), BF16)
    else:
        out_spec = pl.BlockSpec((1, tm, tn), lambda b, i, j: (b, i, j))
        out_shape = jax.ShapeDtypeStruct((n_b, s, n), BF16)
    return pl.pallas_call(
        functools.partial(_proj_kernel, mode=mode, tn=tn, scale=scale),
        grid=(n_b, s // tm, n // tn),
        in_specs=in_specs,
        out_specs=out_spec,
        out_shape=out_shape,
        compiler_params=_params(("parallel", "parallel", "arbitrary")),
        name="proj_" + mode,
    )(*args)


def _proj_classes_kernel(u_ref, w_ref, *rest, norm, scale):
    if norm:
        gain_ref, rest = rest[0], rest[1:]
    o_refs, scr = rest[:N_DIL_GROUPS], rest[N_DIL_GROUPS]
    acc = jnp.dot(u_ref[0], w_ref[...], preferred_element_type=F32)
    tm = acc.shape[0]
    for g, (_, dil) in enumerate(DIL_GROUPS):
        for h in range(HEADS_PER_DIL):
            head = g * HEADS_PER_DIL + h
            lanes = slice(h * HEAD_DIM, (h + 1) * HEAD_DIM)
            y = acc[:, head * HEAD_DIM:(head + 1) * HEAD_DIM]
            if norm:
                y = _rms_head(y, gain_ref[...])
            if scale != 1.0:
                y = y * scale
            if dil == 1:
                o_refs[g][0, 0, :, lanes] = y.astype(BF16)
                continue
            scr[head] = y
            for r in range(dil):
                o_refs[g][0, r, :, lanes] = scr[head, pl.ds(r, tm // dil, stride=dil), :].astype(BF16)


def _proj_classes(u, w, gain=None, scale=1.0):
    n_b, s, d = u.shape
    tm = PROJ_TM
    norm = gain is not None
    in_specs = [
        pl.BlockSpec((1, tm, d), lambda b, i: (b, i, 0)),
        pl.BlockSpec((d, D_A), lambda b, i: (0, 0)),
    ]
    args = [u, w]
    if norm:
        in_specs.append(pl.BlockSpec((1, HEAD_DIM), lambda b, i: (0, 0)))
        args.append(gain.reshape(1, HEAD_DIM))
    return pl.pallas_call(
        functools.partial(_proj_classes_kernel, norm=norm, scale=scale),
        grid=(n_b, s // tm),
        in_specs=in_specs,
        out_specs=[pl.BlockSpec((1, dil, tm // dil, D_A_OUT), lambda b, i: (b, 0, i, 0))
                   for _, dil in DIL_GROUPS],
        out_shape=[jax.ShapeDtypeStruct((n_b, dil, s // dil, D_A_OUT), BF16) for _, dil in DIL_GROUPS],
        scratch_shapes=[pltpu.VMEM((N_HEADS_A, tm, HEAD_DIM), F32)],
        compiler_params=_params(("parallel", "arbitrary")),
        name="proj_classes",
    )(*args)


GQA_TQ = 512
GQA_TK = 512
LOG2E = 1.4426950408889634


def _gqa_kernel(q_ref, k_ref, vt_ref, *rest, n_cast):
    f32_refs, o_ref = rest[:n_cast], rest[n_cast]
    bf16_refs = rest[n_cast + 1:2 * n_cast + 1]
    acc_ref, s_ref = rest[2 * n_cast + 1:]
    for src, dst in zip(f32_refs, bf16_refs):
        dst[...] = src[...].astype(BF16)
    n_chunks = vt_ref.shape[2]
    tq = q_ref.shape[1]
    heads = range(Q_PER_KV)
    nt_dims = (((1,), (1,)), ((), ()))

    def k_chunk(c):
        return k_ref[0, pl.ds(pl.multiple_of(c * GQA_TK, GQA_TK), GQA_TK), :]

    def q_head(g):
        return q_ref[0, :, g * HEAD_DIM:(g + 1) * HEAD_DIM]

    def step(c, c_next, slot, ms, ls):
        if c_next is not None:
            k_next = k_chunk(c_next)
        vt = vt_ref[0, 0, c]
        new_ms, new_ls = [], []
        for g in heads:
            if c_next is not None:
                s_ref[1 - slot, g] = lax.dot_general(k_next, q_head(g), nt_dims, preferred_element_type=F32)
            m_new = jnp.maximum(ms[g], jnp.max(s_ref[slot, g], axis=0, keepdims=True))
            alpha = jnp.exp2(ms[g] - m_new)
            p = jnp.exp2(s_ref[slot, g] - m_new)
            new_ls.append(alpha * ls[g] + jnp.sum(p, axis=0, keepdims=True))
            new_ms.append(m_new)
            acc_ref[g] = alpha * acc_ref[g] + jnp.dot(vt, p.astype(BF16), preferred_element_type=F32)
        return tuple(new_ms), tuple(new_ls)

    acc_ref[...] = jnp.zeros(acc_ref.shape, F32)
    k0 = k_chunk(0)
    for g in heads:
        s_ref[0, g] = lax.dot_general(k0, q_head(g), nt_dims, preferred_element_type=F32)

    def body(i, carry):
        ms, ls = carry
        ms, ls = step(2 * i, 2 * i + 1, 0, ms, ls)
        return step(2 * i + 1, 2 * i + 2, 1, ms, ls)

    init = (tuple(jnp.full((1, tq), MASK_VALUE, F32) for _ in heads),
            tuple(jnp.zeros((1, tq), F32) for _ in heads))
    ms, ls = lax.fori_loop(0, n_chunks // 2 - 1, body, init)
    ms, ls = step(n_chunks - 2, n_chunks - 1, 0, ms, ls)
    _, ls = step(n_chunks - 1, None, 1, ms, ls)
    for g in heads:
        o_ref[0, :, g * HEAD_DIM:(g + 1) * HEAD_DIM] = (acc_ref[g] / ls[g]).T.astype(BF16)


BF16_SUBLANES = 16


def _cast_block(n_rows, n_steps):
    per = 1
    while n_rows % (n_steps // per) or (n_rows // (n_steps // per)) % BF16_SUBLANES:
        per *= 2
        assert per <= n_steps
    return n_rows // (n_steps // per), per


def _gqa(qb, kb, vbt, to_cast):
    n_b, s, _ = qb.shape
    tq = GQA_TQ
    w = Q_PER_KV * HEAD_DIM
    nq = s // tq
    n_steps = n_b * N_KV_B * nq
    cast_specs = []
    for arr in to_cast:
        rows, per = _cast_block(arr.shape[0], n_steps)
        cast_specs.append(pl.BlockSpec(
            (rows, arr.shape[1]), lambda b, h, i, per=per: (((b * N_KV_B + h) * nq + i) // per, 0)))
    out = pl.pallas_call(
        functools.partial(_gqa_kernel, n_cast=len(to_cast)),
        grid=(n_b, N_KV_B, nq),
        in_specs=[
            pl.BlockSpec((1, tq, w), lambda b, h, i: (b, i, h)),
            pl.BlockSpec((1, s, HEAD_DIM), lambda b, h, i: (b, 0, h)),
            pl.BlockSpec((1, 1, s // GQA_TK, HEAD_DIM, GQA_TK), lambda b, h, i: (b, h, 0, 0, 0)),
        ] + cast_specs,
        out_specs=[pl.BlockSpec((1, tq, w), lambda b, h, i: (b, i, h))] + cast_specs,
        out_shape=[jax.ShapeDtypeStruct((n_b, s, D_QB), BF16)]
        + [jax.ShapeDtypeStruct(arr.shape, BF16) for arr in to_cast],
        scratch_shapes=[pltpu.VMEM((Q_PER_KV, HEAD_DIM, tq), F32),
                        pltpu.VMEM((2, Q_PER_KV, GQA_TK, tq), F32)],
        compiler_params=_params(("arbitrary", "arbitrary", "arbitrary")),
        name="gqa_attn",
    )(qb, kb, vbt, *to_cast)
    return out[0], out[1:]


DIL_TQ = 512
DIL_SUB = 128
DIL_HALF = 64


def _dil_kernel(q_ref, kp_ref, kc_ref, kn_ref, vp_ref, vc_ref, vn_ref, o_ref, lse_ref, *, group, dil, n_rows):
    i = pl.program_id(2)
    win = DIL_SUB + 2 * DIL_HALF
    row = lax.broadcasted_iota(jnp.int32, (DIL_SUB, win), 0)
    col = lax.broadcasted_iota(jnp.int32, (DIL_SUB, win), 1)
    dist = jnp.abs(col - DIL_HALF - row)
    in_band = dist <= DIL_HALF
    dist_f = dist.astype(F32) * float(dil)
    for h in range(HEADS_PER_DIL):
        sl = slice(h * HEAD_DIM, (h + 1) * HEAD_DIM)
        slope = 2.0 ** (-ALIBI_MAX * (group * HEADS_PER_DIL + h + 1) / N_HEADS_A)
        k = jnp.concatenate([kp_ref[:, sl], kc_ref[:, sl], kn_ref[:, sl]], axis=0)
        v = jnp.concatenate([vp_ref[:, sl], vc_ref[:, sl], vn_ref[:, sl]], axis=0)
        for j in range(DIL_TQ // DIL_SUB):
            rows = slice(j * DIL_SUB, (j + 1) * DIL_SUB)
            k_abs = i * DIL_TQ + (j * DIL_SUB - DIL_HALF) + col
            valid = in_band & (k_abs >= 0) & (k_abs < n_rows)
            s = lax.dot_general(q_ref[rows, sl], k[j * DIL_SUB:j * DIL_SUB + win],
                                (((1,), (1,)), ((), ())), preferred_element_type=F32)
            s = jnp.where(valid, s - slope * dist_f, MASK_VALUE)
            m = jnp.max(s, axis=-1, keepdims=True)
            p = jnp.exp(s - m)
            l = jnp.sum(p, axis=-1, keepdims=True)
            o = jnp.dot(p.astype(BF16), v[j * DIL_SUB:j * DIL_SUB + win], preferred_element_type=F32)
            o_ref[rows, sl] = o / l
            lse_ref[rows, sl] = jnp.broadcast_to(m + jnp.log(l), (DIL_SUB, HEAD_DIM))


def _dilated_group(q, k, v, group, dil):
    n_b, _, n_rows, w = q.shape
    tq = DIL_TQ
    halo_per_blk = tq // DIL_HALF
    n_halo = n_rows // DIL_HALF
    cur = pl.BlockSpec((None, None, tq, w), lambda b, r, i: (b, r, i, 0))
    prev = pl.BlockSpec((None, None, DIL_HALF, w),
                        lambda b, r, i: (b, r, jnp.maximum(i * halo_per_blk - 1, 0), 0))
    nxt = pl.BlockSpec((None, None, DIL_HALF, w),
                       lambda b, r, i: (b, r, jnp.minimum((i + 1) * halo_per_blk, n_halo - 1), 0))
    return pl.pallas_call(
        functools.partial(_dil_kernel, group=group, dil=dil, n_rows=n_rows),
        grid=(n_b, dil, n_rows // tq),
        in_specs=[cur, prev, cur, nxt, prev, cur, nxt],
        out_specs=[cur, cur],
        out_shape=[jax.ShapeDtypeStruct((n_b, dil, n_rows, w), F32)] * 2,
        compiler_params=_params(("parallel", "parallel", "arbitrary")),
        name="dilated_attn_g%d" % group,
    )(q, k, k, k, v, v, v)


MERGE_TM = 256


def _merge_kernel(o0_ref, o1_ref, o2_ref, l0_ref, l1_ref, l2_ref, ob_ref, sa_ref, sb_ref,
                  wa_ref, wb_ref, wo_ref, h_ref, g_ref, out_ref, nat_ref):
    tm = out_ref.shape[1]

    def natural(ref, dil, slot):
        if dil == 1:
            return ref[0]
        for r in range(dil):
            for h in range(HEADS_PER_DIL):
                nat_ref[slot, h, pl.ds(r, tm // dil, stride=dil), :] = ref[r, :, h * HEAD_DIM:(h + 1) * HEAD_DIM]
        return jnp.concatenate([nat_ref[slot, h] for h in range(HEADS_PER_DIL)], axis=1)

    dils = [dil for _, dil in DIL_GROUPS]
    l0, l1, l2 = [natural(ref, dil, n) for n, (ref, dil) in enumerate(zip((l0_ref, l1_ref, l2_ref), dils))]
    o0, o1, o2 = [natural(ref, dil, N_DIL_GROUPS + n)
                  for n, (ref, dil) in enumerate(zip((o0_ref, o1_ref, o2_ref), dils))]
    mx = jnp.maximum(jnp.maximum(l0, l1), l2)
    e0, e1, e2 = jnp.exp(l0 - mx), jnp.exp(l1 - mx), jnp.exp(l2 - mx)
    oa = (e0 * o0 + e1 * o1 + e2 * o2) / (e0 + e1 + e2)
    pa = jnp.dot(oa.astype(BF16), wa_ref[...], preferred_element_type=F32)
    pb = jnp.dot(ob_ref[0], wb_ref[...], preferred_element_type=F32)
    merged = sa_ref[0].astype(F32) * pa + sb_ref[0].astype(F32) * pb
    mixed = jnp.dot(merged.astype(BF16), wo_ref[...], preferred_element_type=F32)
    out_ref[0] = h_ref[0] + g_ref[0, 0] * mixed


def _merge(o_groups, lse_groups, out_b, gates, wa, wb, wo, h, mod4, k_gate):
    n_b, s, d = h.shape
    tm = MERGE_TM
    a_specs = [pl.BlockSpec((None, dil, tm // dil, D_A_OUT), lambda b, i: (b, 0, i, 0))
               for _, dil in DIL_GROUPS]
    row_spec = pl.BlockSpec((1, tm, d), lambda b, i: (b, i, 0))
    full = lambda shape: pl.BlockSpec(shape, lambda b, i: (0, 0))
    return pl.pallas_call(
        _merge_kernel,
        grid=(n_b, s // tm),
        in_specs=a_specs + a_specs + [
            pl.BlockSpec((1, tm, D_QB), lambda b, i: (b, i, 0)),
            pl.BlockSpec((1, tm, d), lambda b, i: (b, i, 0)),
            pl.BlockSpec((1, tm, d), lambda b, i: (b, i, 1)),
            full((D_A_OUT, d)), full((D_QB, d)), full((d, d)),
            row_spec,
            pl.BlockSpec((1, 1, 1, d), lambda b, i: (b, k_gate, 0, 0)),
        ],
        out_specs=row_spec,
        out_shape=jax.ShapeDtypeStruct((n_b, s, d), F32),
        scratch_shapes=[pltpu.VMEM((2 * N_DIL_GROUPS, HEADS_PER_DIL, tm, HEAD_DIM), F32)],
        compiler_params=_params(("parallel", "arbitrary")),
        name="mix_merge",
    )(*o_groups, *lse_groups, out_b, gates, gates, wa, wb, wo, h, mod4)


def _rope_tables(s):
    rows = s // GRID_W
    row = jnp.repeat(jnp.arange(rows), GRID_W).astype(F32)
    col = jnp.tile(jnp.arange(GRID_W), rows).astype(F32)
    half = HEAD_DIM // 2
    inv_freq = ROPE_THETA ** (-jnp.arange(0, half, 2, dtype=F32) / half)
    ang = jnp.concatenate([row[:, None] * inv_freq, col[:, None] * inv_freq], axis=-1)
    cos, sin = jnp.cos(ang), jnp.sin(ang)
    return jnp.concatenate([cos, cos], axis=-1), jnp.concatenate([-sin, sin], axis=-1)


def kernel(x, c, w_ada, b_ada, norm_ffn1, w1_ffn1, w3_ffn1, w2_ffn1, norm_mix, w_in, q_norm_a, k_norm_a, q_norm_b, k_norm_b, w_branch_a, w_branch_b, w_out, norm_ffn2, w1_ffn2, w3_ffn2, w2_ffn2, norm_final):
    n_b, s, d = x.shape
    assert w_ada.shape[0] == 1, "the fused pipeline covers a single layer"
    cos_f, sin_f = _rope_tables(s)
    h = x
    for l in range(1):
        mod4 = _ada(c, w_ada[l], b_ada[l]).reshape(n_b, N_MOD, 1, d)
        h, u = _ffn(h, mod4, norm_ffn1[l], w1_ffn1[l].astype(BF16), w3_ffn1[l].astype(BF16),
                    w2_ffn1[l].astype(BF16), (0, 1, 2), "mid", norm_mix[l], (3, 4))
        wi = w_in[l]
        o_qk = 3 * D_A
        gain_b = jnp.concatenate([jnp.tile(q_norm_b[l], N_Q_B), jnp.tile(k_norm_b[l], N_KV_B)])[None]
        qa = _proj_classes(u, wi[:, :D_A].astype(BF16), q_norm_a[l], scale=SM_SCALE)
        ka = _proj_classes(u, wi[:, D_A:2 * D_A].astype(BF16), k_norm_a[l])
        va = _proj_classes(u, wi[:, 2 * D_A:o_qk].astype(BF16))
        qb = _proj(u, wi[:, o_qk:o_qk + D_QB].astype(BF16), "norm_rope", 1024, gain_b[:, :D_QB],
                   cos_f, sin_f, scale=SM_SCALE * LOG2E)
        kb = _proj(u, wi[:, o_qk + D_QB:o_qk + D_QB + D_KVB].astype(BF16), "norm_rope", 256,
                   gain_b[:, D_QB:], cos_f, sin_f)
        vb = _proj(u, wi[:, o_qk + D_QB + D_KVB:o_qk + D_QB + 2 * D_KVB].astype(BF16), "chunk_t", 256)
        gates = _proj(u, wi[:, o_qk + D_QB + 2 * D_KVB:].astype(BF16), "sigmoid", 1024)

        o_groups, lse_groups = [], []
        for g, (_, dil) in enumerate(DIL_GROUPS):
            o_g, lse_g = _dilated_group(qa[g], ka[g], va[g], g, dil)
            o_groups.append(o_g)
            lse_groups.append(lse_g)
        out_b, ffn2_w = _gqa(qb, kb, vb, [w1_ffn2[l], w3_ffn2[l], w2_ffn2[l]])
        h = _merge(o_groups, lse_groups, out_b, gates, w_branch_a[l].astype(BF16),
                   w_branch_b[l].astype(BF16), w_out[l].astype(BF16), h, mod4, 5)
        h = _ffn(h, mod4, norm_ffn2[l], *ffn2_w, (6, 7, 8), "final", norm_final)
    return h
```

```python
import functools

import jax
import jax.numpy as jnp
from jax import lax
from jax.experimental import pallas as pl
from jax.experimental.pallas import tpu as pltpu

F32 = jnp.float32
BF16 = jnp.bfloat16

D_MODEL = 2048
D_FF = 5632
HEAD_DIM = 128
DIL_GROUPS = ((128, 1), (512, 4), (2048, 16))
N_DIL_GROUPS = 3
HEADS_PER_DIL = 4
N_HEADS_A = N_DIL_GROUPS * HEADS_PER_DIL
N_Q_B = 8
N_KV_B = 2
Q_PER_KV = N_Q_B // N_KV_B
GRID_W = 64
ROPE_THETA = 10000.0
ALIBI_MAX = 8.0
N_MOD = 9
EPS = 1e-6
D_A = N_HEADS_A * HEAD_DIM
D_A_OUT = HEADS_PER_DIL * HEAD_DIM
D_QB = N_Q_B * HEAD_DIM
D_KVB = N_KV_B * HEAD_DIM
SM_SCALE = HEAD_DIM ** -0.5
MASK_VALUE = -1e30

LANES = 128
BF16_SUBLANES = 16
VMEM_LIMIT = 56 * 1024 * 1024


def _params(sem):
    return pltpu.CompilerParams(dimension_semantics=sem, vmem_limit_bytes=VMEM_LIMIT)


def _silu(x):
    return x * jax.nn.sigmoid(x)


def _rms(x, gain):
    ms = jnp.mean(x * x, axis=-1, keepdims=True)
    return x * lax.rsqrt(ms + EPS) * gain


def _rms_head(x, gain):
    sq = x * x
    hi = sq.astype(BF16)
    lo = (sq - hi.astype(F32)).astype(BF16)
    ones = jnp.ones((HEAD_DIM, HEAD_DIM), BF16)
    ssq = (jnp.dot(hi, ones, preferred_element_type=F32) + jnp.dot(lo, ones, preferred_element_type=F32))
    return x * lax.rsqrt(ssq * (1.0 / HEAD_DIM) + EPS) * gain


ADA_TN = 1024


def _ada_kernel(c_ref, w_ref, b_ref, o_ref):
    n_b = c_ref.shape[0]
    for b in range(n_b):
        cs = _silu(c_ref[b])
        for j in range(ADA_TN // LANES):
            sl = slice(j * LANES, (j + 1) * LANES)
            acc = jnp.sum(w_ref[:, sl] * cs, axis=0, keepdims=True)
            o_ref[b, :, sl] = acc + b_ref[:, sl]


def _ada(c, w, b):
    n_b, d = c.shape
    n = w.shape[1]
    c_b = jnp.broadcast_to(c[:, :, None], (n_b, d, LANES))
    return pl.pallas_call(
        _ada_kernel,
        grid=(n // ADA_TN,),
        in_specs=[
            pl.BlockSpec((n_b, d, LANES), lambda j: (0, 0, 0)),
            pl.BlockSpec((d, ADA_TN), lambda j: (0, j)),
            pl.BlockSpec((1, ADA_TN), lambda j: (0, j)),
        ],
        out_specs=pl.BlockSpec((n_b, 1, ADA_TN), lambda j: (0, 0, j)),
        out_shape=jax.ShapeDtypeStruct((n_b, 1, n), F32),
        compiler_params=_params(("arbitrary",)),
        name="ada_mod",
    )(c_b, w, b.reshape(1, n))


FFN_TM = 512
FFN_TF = 512
FFN_ROWS = 256


def _ffn_kernel(x_ref, gain_ref, sh_ref, sc_ref, g_ref, w1_ref, w3_ref, w2_ref, *rest, mode, n_cast):
    if mode == "mid":
        gain2_ref, sh2_ref, sc2_ref = rest[:3]
        f32_refs, (h_ref, u2_ref) = rest[3:3 + n_cast], rest[3 + n_cast:5 + n_cast]
        bf16_refs, (u_scr, acc_scr) = rest[5 + n_cast:5 + 2 * n_cast], rest[5 + 2 * n_cast:]
    else:
        gainf_ref = rest[0]
        f32_refs, y_ref = rest[1:1 + n_cast], rest[1 + n_cast]
        bf16_refs, (u_scr, acc_scr) = rest[2 + n_cast:2 + 2 * n_cast], rest[2 + 2 * n_cast:]
    for src, dst in zip(f32_refs, bf16_refs):
        dst[...] = src[...].astype(BF16)
    f = pl.program_id(2)
    nf = pl.num_programs(2)
    tm = x_ref.shape[1]
    row_chunks = [slice(r, r + FFN_ROWS) for r in range(0, tm, FFN_ROWS)]

    def swiglu_part(u):
        a = jnp.dot(u, w1_ref[...], preferred_element_type=F32)
        b = jnp.dot(u, w3_ref[...], preferred_element_type=F32)
        hm = (_silu(a) * b).astype(BF16)
        return jnp.dot(hm, w2_ref[...], preferred_element_type=F32)

    @pl.when(f == 0)
    def _():
        for rows in row_chunks:
            y = _rms(x_ref[0, rows], gain_ref[...])
            u = (y * (1.0 + sc_ref[0, 0]) + sh_ref[0, 0]).astype(BF16)
            u_scr[rows] = u
            acc_scr[rows] = swiglu_part(u)

    @pl.when((f > 0) & (f < nf - 1))
    def _():
        acc_scr[...] += swiglu_part(u_scr[...])

    @pl.when(f == nf - 1)
    def _():
        for rows in row_chunks:
            total = acc_scr[rows] + swiglu_part(u_scr[rows])
            h = x_ref[0, rows] + (0.5 * g_ref[0, 0]) * total
            if mode == "mid":
                h_ref[0, rows] = h
                y2 = _rms(h, gain2_ref[...])
                u2_ref[0, rows] = (y2 * (1.0 + sc2_ref[0, 0]) + sh2_ref[0, 0]).astype(BF16)
            else:
                y_ref[0, rows] = _rms(h, gainf_ref[...])


def _mod_spec(k):
    return pl.BlockSpec((1, 1, 1, D_MODEL), lambda b, i, f, k=k: (b, k, 0, 0))


def _ffn_cast_spec(shape, n_b, n_i, n_f):
    rows = shape[0] // (n_b * n_i)
    assert rows * n_b * n_i == shape[0] and rows % BF16_SUBLANES == 0
    lane_blocks = shape[1] // LANES
    n_col = max(k for k in range(1, n_f + 1) if lane_blocks % k == 0)
    return pl.BlockSpec((rows, shape[1] // n_col),
                        lambda b, i, f: (b * n_i + i, jnp.minimum(f, n_col - 1)))


def _ffn(x, mod4, gain, w1, w3, w2, ks, mode, gain_next, ks_next=None, to_cast=()):
    n_b, s, d = x.shape
    tm, tf = FFN_TM, FFN_TF
    grid = (n_b, s // tm, D_FF // tf)
    cast_specs = [_ffn_cast_spec(arr.shape, *grid) for arr in to_cast]
    cast_shapes = [jax.ShapeDtypeStruct(arr.shape, BF16) for arr in to_cast]
    row_spec = pl.BlockSpec((1, tm, d), lambda b, i, f: (b, i, 0))
    vec_spec = pl.BlockSpec((1, d), lambda b, i, f: (0, 0))
    in_specs = [
        row_spec, vec_spec, _mod_spec(ks[0]), _mod_spec(ks[1]), _mod_spec(ks[2]),
        pl.BlockSpec((d, tf), lambda b, i, f: (0, f)),
        pl.BlockSpec((d, tf), lambda b, i, f: (0, f)),
        pl.BlockSpec((tf, d), lambda b, i, f: (f, 0)),
        vec_spec,
    ]
    args = [x, gain.reshape(1, d), mod4, mod4, mod4, w1, w3, w2, gain_next.reshape(1, d)]
    if mode == "mid":
        in_specs += [_mod_spec(ks_next[0]), _mod_spec(ks_next[1])]
        args += [mod4, mod4]
        out_specs = [row_spec, row_spec]
        out_shape = [jax.ShapeDtypeStruct((n_b, s, d), F32), jax.ShapeDtypeStruct((n_b, s, d), BF16)]
    else:
        out_specs = [row_spec]
        out_shape = [jax.ShapeDtypeStruct((n_b, s, d), F32)]
    return pl.pallas_call(
        functools.partial(_ffn_kernel, mode=mode, n_cast=len(to_cast)),
        grid=grid,
        in_specs=in_specs + cast_specs,
        out_specs=out_specs + cast_specs,
        out_shape=out_shape + cast_shapes,
        scratch_shapes=[pltpu.VMEM((tm, d), BF16), pltpu.VMEM((tm, d), F32)],
        compiler_params=_params(("arbitrary", "arbitrary", "arbitrary")),
        name="ffn_" + mode,
    )(*args, *to_cast)


PROJ_TM = 1024


def _proj_kernel(u_ref, w_ref, *rest, mode, tn, scale):
    if mode == "norm":
        gain_ref, o_ref = rest
    elif mode == "norm_rope":
        gain_ref, cos_ref, sin_ref, o_ref = rest
    else:
        (o_ref,) = rest
    acc = jnp.dot(u_ref[0], w_ref[...], preferred_element_type=F32)
    tm = acc.shape[0]
    if mode == "plain":
        o_ref[0] = acc.astype(BF16)
    elif mode == "chunk_t":
        acc_t = acc.T.astype(BF16)
        for hh in range(tn // HEAD_DIM):
            for cc in range(tm // GQA_TK):
                o_ref[0, hh, cc] = acc_t[hh * HEAD_DIM:(hh + 1) * HEAD_DIM, cc * GQA_TK:(cc + 1) * GQA_TK]
    elif mode == "sigmoid":
        o_ref[0] = jax.nn.sigmoid(acc).astype(BF16)
    else:
        for j in range(tn // HEAD_DIM):
            sl = slice(j * HEAD_DIM, (j + 1) * HEAD_DIM)
            y = _rms_head(acc[:, sl], gain_ref[:, sl])
            if mode == "norm_rope":
                y = y * cos_ref[...] + pltpu.roll(y, HEAD_DIM // 2, axis=1) * sin_ref[...]
            if scale != 1.0:
                y = y * scale
            o_ref[0, :, sl] = y.astype(BF16)


def _proj(u, w, col0, n, mode, tn, gain_row=None, cos=None, sin=None, scale=1.0):
    n_b, s, d = u.shape
    tm = PROJ_TM
    assert col0 % tn == 0 and n % tn == 0
    in_specs = [
        pl.BlockSpec((1, tm, d), lambda b, i, j: (b, i, 0)),
        pl.BlockSpec((d, tn), lambda b, i, j: (0, col0 // tn + j)),
    ]
    args = [u, w]
    if mode in ("norm", "norm_rope"):
        in_specs.append(pl.BlockSpec((1, tn), lambda b, i, j: (0, j)))
        args.append(gain_row)
    if mode == "norm_rope":
        in_specs += [pl.BlockSpec((tm, HEAD_DIM), lambda b, i, j: (i, 0))] * 2
        args += [cos, sin]
    if mode == "chunk_t":
        assert n == tn
        heads = n // HEAD_DIM
        out_spec = pl.BlockSpec((1, heads, tm // GQA_TK, HEAD_DIM, GQA_TK), lambda b, i, j: (b, 0, i, 0, 0))
        out_shape = jax.ShapeDtypeStruct((n_b, heads, s // GQA_TK, HEAD_DIM, GQA_TK), BF16)
    else:
        out_spec = pl.BlockSpec((1, tm, tn), lambda b, i, j: (b, i, j))
        out_shape = jax.ShapeDtypeStruct((n_b, s, n), BF16)
    return pl.pallas_call(
        functools.partial(_proj_kernel, mode=mode, tn=tn, scale=scale),
        grid=(n_b, s // tm, n // tn),
        in_specs=in_specs,
        out_specs=out_spec,
        out_shape=out_shape,
        compiler_params=_params(("parallel", "parallel", "arbitrary")),
        name="proj_" + mode,
    )(*args)


def _proj_classes_kernel(u_ref, w_ref, *rest, norm, scale):
    if norm:
        gain_ref, rest = rest[0], rest[1:]
    o_refs, scr = rest[:N_DIL_GROUPS], rest[N_DIL_GROUPS]
    acc = jnp.dot(u_ref[0], w_ref[...], preferred_element_type=F32)
    tm = acc.shape[0]
    for g, (_, dil) in enumerate(DIL_GROUPS):
        for h in range(HEADS_PER_DIL):
            head = g * HEADS_PER_DIL + h
            lanes = slice(h * HEAD_DIM, (h + 1) * HEAD_DIM)
            y = acc[:, head * HEAD_DIM:(head + 1) * HEAD_DIM]
            if norm:
                y = _rms_head(y, gain_ref[...])
            if scale != 1.0:
                y = y * scale
            if dil == 1:
                o_refs[g][0, 0, :, lanes] = y.astype(BF16)
                continue
            scr[head] = y
            for r in range(dil):
                o_refs[g][0, r, :, lanes] = scr[head, pl.ds(r, tm // dil, stride=dil), :].astype(BF16)


def _proj_classes(u, w, which, gain=None, scale=1.0):
    n_b, s, d = u.shape
    tm = PROJ_TM
    norm = gain is not None
    in_specs = [
        pl.BlockSpec((1, tm, d), lambda b, i: (b, i, 0)),
        pl.BlockSpec((d, D_A), lambda b, i: (0, which)),
    ]
    args = [u, w]
    if norm:
        in_specs.append(pl.BlockSpec((1, HEAD_DIM), lambda b, i: (0, 0)))
        args.append(gain.reshape(1, HEAD_DIM))
    return pl.pallas_call(
        functools.partial(_proj_classes_kernel, norm=norm, scale=scale),
        grid=(n_b, s // tm),
        in_specs=in_specs,
        out_specs=[pl.BlockSpec((1, dil, tm // dil, D_A_OUT), lambda b, i: (b, 0, i, 0))
                   for _, dil in DIL_GROUPS],
        out_shape=[jax.ShapeDtypeStruct((n_b, dil, s // dil, D_A_OUT), BF16) for _, dil in DIL_GROUPS],
        scratch_shapes=[pltpu.VMEM((N_HEADS_A, tm, HEAD_DIM), F32)],
        compiler_params=_params(("parallel", "arbitrary")),
        name="proj_classes",
    )(*args)


GQA_TQ = 512
GQA_TK = 512
LOG2E = 1.4426950408889634


def _gqa_kernel(q_ref, k_ref, vt_ref, *rest, n_cast):
    f32_refs, o_ref = rest[:n_cast], rest[n_cast]
    bf16_refs = rest[n_cast + 1:2 * n_cast + 1]
    acc_ref, s_ref = rest[2 * n_cast + 1:]
    for src, dst in zip(f32_refs, bf16_refs):
        dst[...] = src[...].astype(BF16)
    n_chunks = vt_ref.shape[2]
    tq = q_ref.shape[1]
    heads = range(Q_PER_KV)
    nt_dims = (((1,), (1,)), ((), ()))

    def k_chunk(c):
        return k_ref[0, pl.ds(pl.multiple_of(c * GQA_TK, GQA_TK), GQA_TK), :]

    def q_head(g):
        return q_ref[0, :, g * HEAD_DIM:(g + 1) * HEAD_DIM]

    ones_rows = jnp.ones((BF16_SUBLANES, GQA_TK), BF16)

    def step(c, c_next, slot, ms):
        if c_next is not None:
            k_next = k_chunk(c_next)
        vt = jnp.concatenate([vt_ref[0, 0, c], ones_rows], axis=0)
        new_ms = []
        for g in heads:
            if c_next is not None:
                s_ref[1 - slot, g] = lax.dot_general(k_next, q_head(g), nt_dims, preferred_element_type=F32)
            m_new = jnp.maximum(ms[g], jnp.max(s_ref[slot, g], axis=0, keepdims=True))
            alpha = jnp.exp2(ms[g] - m_new)
            p = jnp.exp2(s_ref[slot, g] - m_new).astype(BF16)
            new_ms.append(m_new)
            acc_ref[g] = alpha * acc_ref[g] + jnp.dot(vt, p, preferred_element_type=F32)
        return tuple(new_ms)

    acc_ref[...] = jnp.zeros(acc_ref.shape, F32)
    k0 = k_chunk(0)
    for g in heads:
        s_ref[0, g] = lax.dot_general(k0, q_head(g), nt_dims, preferred_element_type=F32)

    def body(i, ms):
        ms = step(2 * i, 2 * i + 1, 0, ms)
        return step(2 * i + 1, 2 * i + 2, 1, ms)

    init = tuple(jnp.full((1, tq), MASK_VALUE, F32) for _ in heads)
    ms = lax.fori_loop(0, n_chunks // 2 - 1, body, init)
    ms = step(n_chunks - 2, n_chunks - 1, 0, ms)
    step(n_chunks - 1, None, 1, ms)
    for g in heads:
        o_t = acc_ref[g, :HEAD_DIM] / acc_ref[g, HEAD_DIM:HEAD_DIM + 1]
        o_ref[0, :, g * HEAD_DIM:(g + 1) * HEAD_DIM] = o_t.T.astype(BF16)


def _cast_block(n_rows, n_steps):
    per = 1
    while n_rows % (n_steps // per) or (n_rows // (n_steps // per)) % BF16_SUBLANES:
        per *= 2
        assert per <= n_steps
    return n_rows // (n_steps // per), per


def _gqa(qb, kb, vbt, to_cast):
    n_b, s, _ = qb.shape
    tq = GQA_TQ
    w = Q_PER_KV * HEAD_DIM
    nq = s // tq
    n_steps = n_b * N_KV_B * nq
    cast_specs = []
    for arr in to_cast:
        rows, per = _cast_block(arr.shape[0], n_steps)
        cast_specs.append(pl.BlockSpec(
            (rows, arr.shape[1]), lambda b, h, i, per=per: (((b * N_KV_B + h) * nq + i) // per, 0)))
    out = pl.pallas_call(
        functools.partial(_gqa_kernel, n_cast=len(to_cast)),
        grid=(n_b, N_KV_B, nq),
        in_specs=[
            pl.BlockSpec((1, tq, w), lambda b, h, i: (b, i, h)),
            pl.BlockSpec((1, s, HEAD_DIM), lambda b, h, i: (b, 0, h)),
            pl.BlockSpec((1, 1, s // GQA_TK, HEAD_DIM, GQA_TK), lambda b, h, i: (b, h, 0, 0, 0)),
        ] + cast_specs,
        out_specs=[pl.BlockSpec((1, tq, w), lambda b, h, i: (b, i, h))] + cast_specs,
        out_shape=[jax.ShapeDtypeStruct((n_b, s, D_QB), BF16)]
        + [jax.ShapeDtypeStruct(arr.shape, BF16) for arr in to_cast],
        scratch_shapes=[pltpu.VMEM((Q_PER_KV, HEAD_DIM + BF16_SUBLANES, tq), F32),
                        pltpu.VMEM((2, Q_PER_KV, GQA_TK, tq), F32)],
        compiler_params=_params(("arbitrary", "arbitrary", "arbitrary")),
        name="gqa_attn",
    )(qb, kb, vbt, *to_cast)
    return out[0], out[1:]


DIL_TQ = 512
DIL_SUB = 128
DIL_HALF = 64


def _dil_kernel(q_ref, kp_ref, kc_ref, kn_ref, vp_ref, vc_ref, vn_ref, o_ref, lse_ref, *, group, dil, n_rows):
    i = pl.program_id(2)
    win = DIL_SUB + 2 * DIL_HALF
    row = lax.broadcasted_iota(jnp.int32, (DIL_SUB, win), 0)
    col = lax.broadcasted_iota(jnp.int32, (DIL_SUB, win), 1)
    dist = jnp.abs(col - DIL_HALF - row)
    in_band = dist <= DIL_HALF
    dist_f = dist.astype(F32) * float(dil)
    for h in range(HEADS_PER_DIL):
        sl = slice(h * HEAD_DIM, (h + 1) * HEAD_DIM)
        slope = 2.0 ** (-ALIBI_MAX * (group * HEADS_PER_DIL + h + 1) / N_HEADS_A)
        k = jnp.concatenate([kp_ref[:, sl], kc_ref[:, sl], kn_ref[:, sl]], axis=0)
        v = jnp.concatenate([vp_ref[:, sl], vc_ref[:, sl], vn_ref[:, sl]], axis=0)
        for j in range(DIL_TQ // DIL_SUB):
            rows = slice(j * DIL_SUB, (j + 1) * DIL_SUB)
            k_abs = i * DIL_TQ + (j * DIL_SUB - DIL_HALF) + col
            valid = in_band & (k_abs >= 0) & (k_abs < n_rows)
            s = lax.dot_general(q_ref[rows, sl], k[j * DIL_SUB:j * DIL_SUB + win],
                                (((1,), (1,)), ((), ())), preferred_element_type=F32)
            s = jnp.where(valid, s - slope * dist_f, MASK_VALUE)
            m = jnp.max(s, axis=-1, keepdims=True)
            p = jnp.exp(s - m)
            l = jnp.sum(p, axis=-1, keepdims=True)
            o = jnp.dot(p.astype(BF16), v[j * DIL_SUB:j * DIL_SUB + win], preferred_element_type=F32)
            o_ref[rows, sl] = o / l
            lse_ref[rows, sl] = jnp.broadcast_to(m + jnp.log(l), (DIL_SUB, HEAD_DIM))


def _dilated_group(q, k, v, group, dil):
    n_b, _, n_rows, w = q.shape
    tq = DIL_TQ
    halo_per_blk = tq // DIL_HALF
    n_halo = n_rows // DIL_HALF
    cur = pl.BlockSpec((None, None, tq, w), lambda b, r, i: (b, r, i, 0))
    prev = pl.BlockSpec((None, None, DIL_HALF, w),
                        lambda b, r, i: (b, r, jnp.maximum(i * halo_per_blk - 1, 0), 0))
    nxt = pl.BlockSpec((None, None, DIL_HALF, w),
                       lambda b, r, i: (b, r, jnp.minimum((i + 1) * halo_per_blk, n_halo - 1), 0))
    return pl.pallas_call(
        functools.partial(_dil_kernel, group=group, dil=dil, n_rows=n_rows),
        grid=(n_b, dil, n_rows // tq),
        in_specs=[cur, prev, cur, nxt, prev, cur, nxt],
        out_specs=[cur, cur],
        out_shape=[jax.ShapeDtypeStruct((n_b, dil, n_rows, w), F32)] * 2,
        compiler_params=_params(("parallel", "parallel", "arbitrary")),
        name="dilated_attn_g%d" % group,
    )(q, k, k, k, v, v, v)


MERGE_TM = 256


def _merge_kernel(o0_ref, o1_ref, o2_ref, l0_ref, l1_ref, l2_ref, ob_ref, sa_ref, sb_ref,
                  wa_ref, wb_ref, wo_ref, h_ref, g_ref, out_ref, nat_ref):
    tm = out_ref.shape[1]

    def natural(ref, dil, slot):
        if dil == 1:
            return ref[0]
        for r in range(dil):
            for h in range(HEADS_PER_DIL):
                nat_ref[slot, h, pl.ds(r, tm // dil, stride=dil), :] = ref[r, :, h * HEAD_DIM:(h + 1) * HEAD_DIM]
        return jnp.concatenate([nat_ref[slot, h] for h in range(HEADS_PER_DIL)], axis=1)

    dils = [dil for _, dil in DIL_GROUPS]
    l0, l1, l2 = [natural(ref, dil, n) for n, (ref, dil) in enumerate(zip((l0_ref, l1_ref, l2_ref), dils))]
    o0, o1, o2 = [natural(ref, dil, N_DIL_GROUPS + n)
                  for n, (ref, dil) in enumerate(zip((o0_ref, o1_ref, o2_ref), dils))]
    mx = jnp.maximum(jnp.maximum(l0, l1), l2)
    e0, e1, e2 = jnp.exp(l0 - mx), jnp.exp(l1 - mx), jnp.exp(l2 - mx)
    oa = (e0 * o0 + e1 * o1 + e2 * o2) / (e0 + e1 + e2)
    pa = jnp.dot(oa.astype(BF16), wa_ref[...], preferred_element_type=F32)
    pb = jnp.dot(ob_ref[0], wb_ref[...], preferred_element_type=F32)
    merged = sa_ref[0].astype(F32) * pa + sb_ref[0].astype(F32) * pb
    mixed = jnp.dot(merged.astype(BF16), wo_ref[...], preferred_element_type=F32)
    out_ref[0] = h_ref[0] + g_ref[0, 0] * mixed


def _merge(o_groups, lse_groups, out_b, gates, wa, wb, wo, h, mod4, k_gate):
    n_b, s, d = h.shape
    tm = MERGE_TM
    a_specs = [pl.BlockSpec((None, dil, tm // dil, D_A_OUT), lambda b, i: (b, 0, i, 0))
               for _, dil in DIL_GROUPS]
    row_spec = pl.BlockSpec((1, tm, d), lambda b, i: (b, i, 0))
    full = lambda shape: pl.BlockSpec(shape, lambda b, i: (0, 0))
    return pl.pallas_call(
        _merge_kernel,
        grid=(n_b, s // tm),
        in_specs=a_specs + a_specs + [
            pl.BlockSpec((1, tm, D_QB), lambda b, i: (b, i, 0)),
            pl.BlockSpec((1, tm, d), lambda b, i: (b, i, 0)),
            pl.BlockSpec((1, tm, d), lambda b, i: (b, i, 1)),
            full((D_A_OUT, d)), full((D_QB, d)), full((d, d)),
            row_spec,
            pl.BlockSpec((1, 1, 1, d), lambda b, i: (b, k_gate, 0, 0)),
        ],
        out_specs=row_spec,
        out_shape=jax.ShapeDtypeStruct((n_b, s, d), F32),
        scratch_shapes=[pltpu.VMEM((2 * N_DIL_GROUPS, HEADS_PER_DIL, tm, HEAD_DIM), F32)],
        compiler_params=_params(("parallel", "arbitrary")),
        name="mix_merge",
    )(*o_groups, *lse_groups, out_b, gates, gates, wa, wb, wo, h, mod4)


def _rope_tables(s):
    rows = s // GRID_W
    row = jnp.repeat(jnp.arange(rows), GRID_W).astype(F32)
    col = jnp.tile(jnp.arange(GRID_W), rows).astype(F32)
    half = HEAD_DIM // 2
    inv_freq = ROPE_THETA ** (-jnp.arange(0, half, 2, dtype=F32) / half)
    ang = jnp.concatenate([row[:, None] * inv_freq, col[:, None] * inv_freq], axis=-1)
    cos, sin = jnp.cos(ang), jnp.sin(ang)
    return jnp.concatenate([cos, cos], axis=-1), jnp.concatenate([-sin, sin], axis=-1)


def kernel(x, c, w_ada, b_ada, norm_ffn1, w1_ffn1, w3_ffn1, w2_ffn1, norm_mix, w_in, q_norm_a, k_norm_a, q_norm_b, k_norm_b, w_branch_a, w_branch_b, w_out, norm_ffn2, w1_ffn2, w3_ffn2, w2_ffn2, norm_final):
    n_b, s, d = x.shape
    assert w_ada.shape[0] == 1, "the fused pipeline covers a single layer"
    cos_f, sin_f = _rope_tables(s)
    h = x
    for l in range(1):
        mod4 = _ada(c, w_ada[l], b_ada[l]).reshape(n_b, N_MOD, 1, d)
        h, u, wi, wa, wb, wo = _ffn(
            h, mod4, norm_ffn1[l], w1_ffn1[l].astype(BF16), w3_ffn1[l].astype(BF16),
            w2_ffn1[l].astype(BF16), (0, 1, 2), "mid", norm_mix[l], (3, 4),
            to_cast=[w_in[l], w_branch_a[l], w_branch_b[l], w_out[l]])
        o_qb = 3 * D_A
        o_kb = o_qb + D_QB
        o_vb = o_kb + D_KVB
        o_gates = o_vb + D_KVB
        gain_qb = jnp.tile(q_norm_b[l], N_Q_B)[None]
        gain_kb = jnp.tile(k_norm_b[l], N_KV_B)[None]
        qa = _proj_classes(u, wi, 0, q_norm_a[l], scale=SM_SCALE)
        ka = _proj_classes(u, wi, 1, k_norm_a[l])
        va = _proj_classes(u, wi, 2)
        qb = _proj(u, wi, o_qb, D_QB, "norm_rope", 512, gain_qb, cos_f, sin_f, scale=SM_SCALE * LOG2E)
        kb = _proj(u, wi, o_kb, D_KVB, "norm_rope", 256, gain_kb, cos_f, sin_f)
        vb = _proj(u, wi, o_vb, D_KVB, "chunk_t", 256)
        gates = _proj(u, wi, o_gates, 2 * D_MODEL, "sigmoid", 1024)

        o_groups, lse_groups = [], []
        for g, (_, dil) in enumerate(DIL_GROUPS):
            o_g, lse_g = _dilated_group(qa[g], ka[g], va[g], g, dil)
            o_groups.append(o_g)
            lse_groups.append(lse_g)
        out_b, ffn2_w = _gqa(qb, kb, vb, [w1_ffn2[l], w3_ffn2[l], w2_ffn2[l]])
        h = _merge(o_groups, lse_groups, out_b, gates, wa, wb, wo, h, mod4, 5)
        (h,) = _ffn(h, mod4, norm_ffn2[l], *ffn2_w, (6, 7, 8), "final", norm_final)
    return h
```

```python
import functools

import jax
import jax.numpy as jnp
from jax import lax
from jax.experimental import pallas as pl
from jax.experimental.pallas import tpu as pltpu

F32 = jnp.float32
BF16 = jnp.bfloat16

D_MODEL = 2048
D_FF = 5632
HEAD_DIM = 128
DIL_GROUPS = ((128, 1), (512, 4), (2048, 16))
N_DIL_GROUPS = 3
HEADS_PER_DIL = 4
N_HEADS_A = N_DIL_GROUPS * HEADS_PER_DIL
N_Q_B = 8
N_KV_B = 2
Q_PER_KV = N_Q_B // N_KV_B
GRID_W = 64
ROPE_THETA = 10000.0
ALIBI_MAX = 8.0
N_MOD = 9
EPS = 1e-6
D_A = N_HEADS_A * HEAD_DIM
D_A_OUT = HEADS_PER_DIL * HEAD_DIM
D_QB = N_Q_B * HEAD_DIM
D_KVB = N_KV_B * HEAD_DIM
SM_SCALE = HEAD_DIM ** -0.5
MASK_VALUE = -1e30

LANES = 128
BF16_SUBLANES = 16
VMEM_LIMIT = 56 * 1024 * 1024


def _params(sem):
    return pltpu.CompilerParams(dimension_semantics=sem, vmem_limit_bytes=VMEM_LIMIT)


def _silu(x):
    return x * jax.nn.sigmoid(x)


def _rms(x, gain):
    ms = jnp.mean(x * x, axis=-1, keepdims=True)
    return x * lax.rsqrt(ms + EPS) * gain


def _rms_head(x, gain):
    sq = x * x
    hi = sq.astype(BF16)
    lo = (sq - hi.astype(F32)).astype(BF16)
    ones = jnp.ones((HEAD_DIM, HEAD_DIM), BF16)
    ssq = (jnp.dot(hi, ones, preferred_element_type=F32) + jnp.dot(lo, ones, preferred_element_type=F32))
    return x * lax.rsqrt(ssq * (1.0 / HEAD_DIM) + EPS) * gain


ADA_TN = 1024


def _ada_kernel(c_ref, w_ref, b_ref, o_ref):
    n_b = c_ref.shape[0]
    for b in range(n_b):
        cs = _silu(c_ref[b])
        for j in range(ADA_TN // LANES):
            sl = slice(j * LANES, (j + 1) * LANES)
            acc = jnp.sum(w_ref[:, sl] * cs, axis=0, keepdims=True)
            o_ref[b, :, sl] = acc + b_ref[:, sl]


def _ada(c, w, b):
    n_b, d = c.shape
    n = w.shape[1]
    c_b = jnp.broadcast_to(c[:, :, None], (n_b, d, LANES))
    return pl.pallas_call(
        _ada_kernel,
        grid=(n // ADA_TN,),
        in_specs=[
            pl.BlockSpec((n_b, d, LANES), lambda j: (0, 0, 0)),
            pl.BlockSpec((d, ADA_TN), lambda j: (0, j)),
            pl.BlockSpec((1, ADA_TN), lambda j: (0, j)),
        ],
        out_specs=pl.BlockSpec((n_b, 1, ADA_TN), lambda j: (0, 0, j)),
        out_shape=jax.ShapeDtypeStruct((n_b, 1, n), F32),
        compiler_params=_params(("arbitrary",)),
        name="ada_mod",
    )(c_b, w, b.reshape(1, n))


FFN_TM = 512
FFN_TF = 512
FFN_FIRST_ROWS = (256, 256)
FFN_LAST_ROWS = (256, 256)


def _ffn_kernel(x_ref, gain_ref, sh_ref, sc_ref, g_ref, w1_ref, w3_ref, w2_ref, *rest, mode, n_cast):
    if mode == "mid":
        gain2_ref, sh2_ref, sc2_ref = rest[:3]
        f32_refs, (h_ref, u2_ref) = rest[3:3 + n_cast], rest[3 + n_cast:5 + n_cast]
        bf16_refs, (u_scr, acc_scr) = rest[5 + n_cast:5 + 2 * n_cast], rest[5 + 2 * n_cast:]
    else:
        gainf_ref = rest[0]
        f32_refs, y_ref = rest[1:1 + n_cast], rest[1 + n_cast]
        bf16_refs, (u_scr, acc_scr) = rest[2 + n_cast:2 + 2 * n_cast], rest[2 + 2 * n_cast:]
    for src, dst in zip(f32_refs, bf16_refs):
        dst[...] = src[...].astype(BF16)
    f = pl.program_id(2)
    nf = pl.num_programs(2)
    def chunks(sizes):
        starts = [sum(sizes[:k]) for k in range(len(sizes))]
        return [slice(r, r + n) for r, n in zip(starts, sizes)]

    def swiglu_part(u):
        a = jnp.dot(u, w1_ref[...], preferred_element_type=F32)
        b = jnp.dot(u, w3_ref[...], preferred_element_type=F32)
        hm = (_silu(a) * b).astype(BF16)
        return jnp.dot(hm, w2_ref[...], preferred_element_type=F32)

    @pl.when(f == 0)
    def _():
        gain_mod = gain_ref[...] * (1.0 + sc_ref[0, 0])
        for rows in chunks(FFN_FIRST_ROWS):
            u = (_rms(x_ref[0, rows], gain_mod) + sh_ref[0, 0]).astype(BF16)
            u_scr[rows] = u
            acc_scr[rows] = swiglu_part(u)

    @pl.when((f > 0) & (f < nf - 1))
    def _():
        acc_scr[...] += swiglu_part(u_scr[...])

    @pl.when(f == nf - 1)
    def _():
        if mode == "mid":
            gain2_mod = gain2_ref[...] * (1.0 + sc2_ref[0, 0])
        for rows in chunks(FFN_LAST_ROWS):
            total = acc_scr[rows] + swiglu_part(u_scr[rows])
            h = x_ref[0, rows] + (0.5 * g_ref[0, 0]) * total
            if mode == "mid":
                h_ref[0, rows] = h
                u2_ref[0, rows] = (_rms(h, gain2_mod) + sh2_ref[0, 0]).astype(BF16)
            else:
                y_ref[0, rows] = _rms(h, gainf_ref[...])


def _mod_spec(k):
    return pl.BlockSpec((1, 1, 1, D_MODEL), lambda b, i, f, k=k: (b, k, 0, 0))


def _ffn_cast_spec(shape, n_b, n_i, n_f):
    rows = shape[0] // (n_b * n_i)
    assert rows * n_b * n_i == shape[0] and rows % BF16_SUBLANES == 0
    lane_blocks = shape[1] // LANES
    n_col = max(k for k in range(1, n_f + 1) if lane_blocks % k == 0)
    return pl.BlockSpec((rows, shape[1] // n_col),
                        lambda b, i, f: (b * n_i + i, jnp.minimum(f, n_col - 1)))


def _ffn(x, mod4, gain, w1, w3, w2, ks, mode, gain_next, ks_next=None, to_cast=()):
    n_b, s, d = x.shape
    tm, tf = FFN_TM, FFN_TF
    grid = (n_b, s // tm, D_FF // tf)
    cast_specs = [_ffn_cast_spec(arr.shape, *grid) for arr in to_cast]
    cast_shapes = [jax.ShapeDtypeStruct(arr.shape, BF16) for arr in to_cast]
    row_spec = pl.BlockSpec((1, tm, d), lambda b, i, f: (b, i, 0))
    vec_spec = pl.BlockSpec((1, d), lambda b, i, f: (0, 0))
    in_specs = [
        row_spec, vec_spec, _mod_spec(ks[0]), _mod_spec(ks[1]), _mod_spec(ks[2]),
        pl.BlockSpec((d, tf), lambda b, i, f: (0, f)),
        pl.BlockSpec((d, tf), lambda b, i, f: (0, f)),
        pl.BlockSpec((tf, d), lambda b, i, f: (f, 0)),
        vec_spec,
    ]
    args = [x, gain.reshape(1, d), mod4, mod4, mod4, w1, w3, w2, gain_next.reshape(1, d)]
    if mode == "mid":
        in_specs += [_mod_spec(ks_next[0]), _mod_spec(ks_next[1])]
        args += [mod4, mod4]
        out_specs = [row_spec, row_spec]
        out_shape = [jax.ShapeDtypeStruct((n_b, s, d), F32), jax.ShapeDtypeStruct((n_b, s, d), BF16)]
    else:
        out_specs = [row_spec]
        out_shape = [jax.ShapeDtypeStruct((n_b, s, d), F32)]
    return pl.pallas_call(
        functools.partial(_ffn_kernel, mode=mode, n_cast=len(to_cast)),
        grid=grid,
        in_specs=in_specs + cast_specs,
        out_specs=out_specs + cast_specs,
        out_shape=out_shape + cast_shapes,
        scratch_shapes=[pltpu.VMEM((tm, d), BF16), pltpu.VMEM((tm, d), F32)],
        compiler_params=_params(("arbitrary", "arbitrary", "arbitrary")),
        name="ffn_" + mode,
    )(*args, *to_cast)


PROJ_TM = 1024


def _proj_kernel(u_ref, w_ref, *rest, mode, tn, scale):
    if mode == "norm":
        gain_ref, o_ref = rest
    elif mode == "norm_rope":
        gain_ref, cos_ref, sin_ref, o_ref = rest
    else:
        (o_ref,) = rest
    acc = jnp.dot(u_ref[0], w_ref[...], preferred_element_type=F32)
    tm = acc.shape[0]
    if mode == "plain":
        o_ref[0] = acc.astype(BF16)
    elif mode == "chunk_t":
        acc_t = acc.T.astype(BF16)
        for hh in range(tn // HEAD_DIM):
            for cc in range(tm // GQA_TK):
                o_ref[0, hh, cc] = acc_t[hh * HEAD_DIM:(hh + 1) * HEAD_DIM, cc * GQA_TK:(cc + 1) * GQA_TK]
    elif mode == "sigmoid":
        o_ref[0] = jax.nn.sigmoid(acc).astype(BF16)
    else:
        for j in range(tn // HEAD_DIM):
            sl = slice(j * HEAD_DIM, (j + 1) * HEAD_DIM)
            y = _rms_head(acc[:, sl], gain_ref[:, sl])
            if mode == "norm_rope":
                y = y * cos_ref[...] + pltpu.roll(y, HEAD_DIM // 2, axis=1) * sin_ref[...]
            if scale != 1.0:
                y = y * scale
            o_ref[0, :, sl] = y.astype(BF16)


def _proj(u, w, col0, n, mode, tn, gain_row=None, cos=None, sin=None, scale=1.0):
    n_b, s, d = u.shape
    tm = PROJ_TM
    assert col0 % tn == 0 and n % tn == 0
    in_specs = [
        pl.BlockSpec((1, tm, d), lambda b, i, j: (b, i, 0)),
        pl.BlockSpec((d, tn), lambda b, i, j: (0, col0 // tn + j)),
    ]
    args = [u, w]
    if mode in ("norm", "norm_rope"):
        in_specs.append(pl.BlockSpec((1, tn), lambda b, i, j: (0, j)))
        args.append(gain_row)
    if mode == "norm_rope":
        in_specs += [pl.BlockSpec((tm, HEAD_DIM), lambda b, i, j: (i, 0))] * 2
        args += [cos, sin]
    if mode == "chunk_t":
        assert n == tn
        heads = n // HEAD_DIM
        out_spec = pl.BlockSpec((1, heads, tm // GQA_TK, HEAD_DIM, GQA_TK), lambda b, i, j: (b, 0, i, 0, 0))
        out_shape = jax.ShapeDtypeStruct((n_b, heads, s // GQA_TK, HEAD_DIM, GQA_TK), BF16)
    else:
        out_spec = pl.BlockSpec((1, tm, tn), lambda b, i, j: (b, i, j))
        out_shape = jax.ShapeDtypeStruct((n_b, s, n), BF16)
    return pl.pallas_call(
        functools.partial(_proj_kernel, mode=mode, tn=tn, scale=scale),
        grid=(n_b, s // tm, n // tn),
        in_specs=in_specs,
        out_specs=out_spec,
        out_shape=out_shape,
        compiler_params=_params(("parallel", "parallel", "arbitrary")),
        name="proj_" + mode,
    )(*args)


def _proj_classes_kernel(u_ref, w_ref, *rest, norm, scale):
    if norm:
        gain_ref, rest = rest[0], rest[1:]
    o_refs, scr = rest[:N_DIL_GROUPS], rest[N_DIL_GROUPS]
    acc = jnp.dot(u_ref[0], w_ref[...], preferred_element_type=F32)
    tm = acc.shape[0]
    for g, (_, dil) in enumerate(DIL_GROUPS):
        for h in range(HEADS_PER_DIL):
            head = g * HEADS_PER_DIL + h
            lanes = slice(h * HEAD_DIM, (h + 1) * HEAD_DIM)
            y = acc[:, head * HEAD_DIM:(head + 1) * HEAD_DIM]
            if norm:
                y = _rms_head(y, gain_ref[...])
            if scale != 1.0:
                y = y * scale
            if dil == 1:
                o_refs[g][0, 0, :, lanes] = y.astype(BF16)
                continue
            scr[head] = y
            for r in range(dil):
                o_refs[g][0, r, :, lanes] = scr[head, pl.ds(r, tm // dil, stride=dil), :].astype(BF16)


def _proj_classes(u, w, which, gain=None, scale=1.0):
    n_b, s, d = u.shape
    tm = PROJ_TM
    norm = gain is not None
    in_specs = [
        pl.BlockSpec((1, tm, d), lambda b, i: (b, i, 0)),
        pl.BlockSpec((d, D_A), lambda b, i: (0, which)),
    ]
    args = [u, w]
    if norm:
        in_specs.append(pl.BlockSpec((1, HEAD_DIM), lambda b, i: (0, 0)))
        args.append(gain.reshape(1, HEAD_DIM))
    return pl.pallas_call(
        functools.partial(_proj_classes_kernel, norm=norm, scale=scale),
        grid=(n_b, s // tm),
        in_specs=in_specs,
        out_specs=[pl.BlockSpec((1, dil, tm // dil, D_A_OUT), lambda b, i: (b, 0, i, 0))
                   for _, dil in DIL_GROUPS],
        out_shape=[jax.ShapeDtypeStruct((n_b, dil, s // dil, D_A_OUT), BF16) for _, dil in DIL_GROUPS],
        scratch_shapes=[pltpu.VMEM((N_HEADS_A, tm, HEAD_DIM), F32)],
        compiler_params=_params(("parallel", "arbitrary")),
        name="proj_classes",
    )(*args)


GQA_TQ = 512
GQA_TK = 512
LOG2E = 1.4426950408889634


def _gqa_kernel(q_ref, k_ref, vt_ref, *rest, n_cast):
    f32_refs, o_ref = rest[:n_cast], rest[n_cast]
    bf16_refs = rest[n_cast + 1:2 * n_cast + 1]
    acc_ref, s_ref = rest[2 * n_cast + 1:]
    for src, dst in zip(f32_refs, bf16_refs):
        dst[...] = src[...].astype(BF16)
    n_chunks = vt_ref.shape[2]
    tq = q_ref.shape[1]
    heads = range(Q_PER_KV)
    nt_dims = (((1,), (1,)), ((), ()))

    def k_chunk(c):
        return k_ref[0, pl.ds(pl.multiple_of(c * GQA_TK, GQA_TK), GQA_TK), :]

    def q_head(g):
        return q_ref[0, :, g * HEAD_DIM:(g + 1) * HEAD_DIM]

    ones_rows = jnp.ones((BF16_SUBLANES, GQA_TK), BF16)

    def step(c, c_next, slot, ms):
        if c_next is not None:
            k_next = k_chunk(c_next)
        vt = jnp.concatenate([vt_ref[0, 0, c], ones_rows], axis=0)
        new_ms = []
        for g in heads:
            if c_next is not None:
                s_ref[1 - slot, g] = lax.dot_general(k_next, q_head(g), nt_dims, preferred_element_type=F32)
            m_new = jnp.maximum(ms[g], jnp.max(s_ref[slot, g], axis=0, keepdims=True))
            alpha = jnp.exp2(ms[g] - m_new)
            p = jnp.exp2(s_ref[slot, g] - m_new).astype(BF16)
            new_ms.append(m_new)
            acc_ref[g] = alpha * acc_ref[g] + jnp.dot(vt, p, preferred_element_type=F32)
        return tuple(new_ms)

    acc_ref[...] = jnp.zeros(acc_ref.shape, F32)
    k0 = k_chunk(0)
    for g in heads:
        s_ref[0, g] = lax.dot_general(k0, q_head(g), nt_dims, preferred_element_type=F32)

    def body(i, ms):
        ms = step(2 * i, 2 * i + 1, 0, ms)
        return step(2 * i + 1, 2 * i + 2, 1, ms)

    init = tuple(jnp.full((1, tq), MASK_VALUE, F32) for _ in heads)
    ms = lax.fori_loop(0, n_chunks // 2 - 1, body, init)
    ms = step(n_chunks - 2, n_chunks - 1, 0, ms)
    step(n_chunks - 1, None, 1, ms)
    for g in heads:
        o_t = acc_ref[g, :HEAD_DIM] / acc_ref[g, HEAD_DIM:HEAD_DIM + 1]
        o_ref[0, :, g * HEAD_DIM:(g + 1) * HEAD_DIM] = o_t.T.astype(BF16)


def _cast_block(n_rows, n_steps):
    per = 1
    while n_rows % (n_steps // per) or (n_rows // (n_steps // per)) % BF16_SUBLANES:
        per *= 2
        assert per <= n_steps
    return n_rows // (n_steps // per), per


def _gqa(qb, kb, vbt, to_cast):
    n_b, s, _ = qb.shape
    tq = GQA_TQ
    w = Q_PER_KV * HEAD_DIM
    nq = s // tq
    n_steps = n_b * N_KV_B * nq
    cast_specs = []
    for arr in to_cast:
        rows, per = _cast_block(arr.shape[0], n_steps)
        cast_specs.append(pl.BlockSpec(
            (rows, arr.shape[1]), lambda b, h, i, per=per: (((b * N_KV_B + h) * nq + i) // per, 0)))
    out = pl.pallas_call(
        functools.partial(_gqa_kernel, n_cast=len(to_cast)),
        grid=(n_b, N_KV_B, nq),
        in_specs=[
            pl.BlockSpec((1, tq, w), lambda b, h, i: (b, i, h)),
            pl.BlockSpec((1, s, HEAD_DIM), lambda b, h, i: (b, 0, h)),
            pl.BlockSpec((1, 1, s // GQA_TK, HEAD_DIM, GQA_TK), lambda b, h, i: (b, h, 0, 0, 0)),
        ] + cast_specs,
        out_specs=[pl.BlockSpec((1, tq, w), lambda b, h, i: (b, i, h))] + cast_specs,
        out_shape=[jax.ShapeDtypeStruct((n_b, s, D_QB), BF16)]
        + [jax.ShapeDtypeStruct(arr.shape, BF16) for arr in to_cast],
        scratch_shapes=[pltpu.VMEM((Q_PER_KV, HEAD_DIM + BF16_SUBLANES, tq), F32),
                        pltpu.VMEM((2, Q_PER_KV, GQA_TK, tq), F32)],
        compiler_params=_params(("arbitrary", "arbitrary", "arbitrary")),
        name="gqa_attn",
    )(qb, kb, vbt, *to_cast)
    return out[0], out[1:]


DIL_TQ = 512
DIL_SUB = 128
DIL_HALF = 64


def _dil_kernel(q_ref, kp_ref, kc_ref, kn_ref, vp_ref, vc_ref, vn_ref, o_ref, lse_ref, *, group, dil, n_rows):
    i = pl.program_id(2)
    win = DIL_SUB + 2 * DIL_HALF
    row = lax.broadcasted_iota(jnp.int32, (DIL_SUB, win), 0)
    col = lax.broadcasted_iota(jnp.int32, (DIL_SUB, win), 1)
    dist = jnp.abs(col - DIL_HALF - row)
    in_band = dist <= DIL_HALF
    dist_f = dist.astype(F32) * float(dil)
    for h in range(HEADS_PER_DIL):
        sl = slice(h * HEAD_DIM, (h + 1) * HEAD_DIM)
        slope = 2.0 ** (-ALIBI_MAX * (group * HEADS_PER_DIL + h + 1) / N_HEADS_A)
        k = jnp.concatenate([kp_ref[:, sl], kc_ref[:, sl], kn_ref[:, sl]], axis=0)
        v = jnp.concatenate([vp_ref[:, sl], vc_ref[:, sl], vn_ref[:, sl]], axis=0)
        for j in range(DIL_TQ // DIL_SUB):
            rows = slice(j * DIL_SUB, (j + 1) * DIL_SUB)
            k_abs = i * DIL_TQ + (j * DIL_SUB - DIL_HALF) + col
            valid = in_band & (k_abs >= 0) & (k_abs < n_rows)
            s = lax.dot_general(q_ref[rows, sl], k[j * DIL_SUB:j * DIL_SUB + win],
                                (((1,), (1,)), ((), ())), preferred_element_type=F32)
            s = jnp.where(valid, s - slope * dist_f, MASK_VALUE)
            m = jnp.max(s, axis=-1, keepdims=True)
            p = jnp.exp(s - m)
            l = jnp.sum(p, axis=-1, keepdims=True)
            o = jnp.dot(p.astype(BF16), v[j * DIL_SUB:j * DIL_SUB + win], preferred_element_type=F32)
            o_ref[rows, sl] = (o / l).astype(BF16)
            lse_ref[rows, sl] = jnp.broadcast_to(m + jnp.log(l), (DIL_SUB, HEAD_DIM))


def _dilated_group(q, k, v, group, dil):
    n_b, _, n_rows, w = q.shape
    tq = DIL_TQ
    halo_per_blk = tq // DIL_HALF
    n_halo = n_rows // DIL_HALF
    cur = pl.BlockSpec((None, None, tq, w), lambda b, r, i: (b, r, i, 0))
    prev = pl.BlockSpec((None, None, DIL_HALF, w),
                        lambda b, r, i: (b, r, jnp.maximum(i * halo_per_blk - 1, 0), 0))
    nxt = pl.BlockSpec((None, None, DIL_HALF, w),
                       lambda b, r, i: (b, r, jnp.minimum((i + 1) * halo_per_blk, n_halo - 1), 0))
    return pl.pallas_call(
        functools.partial(_dil_kernel, group=group, dil=dil, n_rows=n_rows),
        grid=(n_b, dil, n_rows // tq),
        in_specs=[cur, prev, cur, nxt, prev, cur, nxt],
        out_specs=[cur, cur],
        out_shape=[jax.ShapeDtypeStruct((n_b, dil, n_rows, w), BF16),
                   jax.ShapeDtypeStruct((n_b, dil, n_rows, w), F32)],
        compiler_params=_params(("parallel", "parallel", "arbitrary")),
        name="dilated_attn_g%d" % group,
    )(q, k, k, k, v, v, v)


MERGE_TM = 256


def _merge_kernel(o0_ref, o1_ref, o2_ref, l0_ref, l1_ref, l2_ref, ob_ref, sa_ref, sb_ref,
                  wa_ref, wb_ref, wo_ref, h_ref, g_ref, out_ref, nat_ref):
    tm = out_ref.shape[1]

    def natural(ref, dil, slot):
        if dil == 1:
            return ref[0].astype(F32)
        for r in range(dil):
            for h in range(HEADS_PER_DIL):
                nat_ref[slot, h, pl.ds(r, tm // dil, stride=dil), :] = (
                    ref[r, :, h * HEAD_DIM:(h + 1) * HEAD_DIM].astype(F32))
        return jnp.concatenate([nat_ref[slot, h] for h in range(HEADS_PER_DIL)], axis=1)

    gated_b = sb_ref[0].astype(F32) * jnp.dot(ob_ref[0], wb_ref[...], preferred_element_type=F32)
    dils = [dil for _, dil in DIL_GROUPS]
    l0, l1, l2 = [natural(ref, dil, n) for n, (ref, dil) in enumerate(zip((l0_ref, l1_ref, l2_ref), dils))]
    o0, o1, o2 = [natural(ref, dil, N_DIL_GROUPS + n)
                  for n, (ref, dil) in enumerate(zip((o0_ref, o1_ref, o2_ref), dils))]
    mx = jnp.maximum(jnp.maximum(l0, l1), l2)
    e0, e1, e2 = jnp.exp(l0 - mx), jnp.exp(l1 - mx), jnp.exp(l2 - mx)
    oa = (e0 * o0 + e1 * o1 + e2 * o2) / (e0 + e1 + e2)
    pa = jnp.dot(oa.astype(BF16), wa_ref[...], preferred_element_type=F32)
    merged = sa_ref[0].astype(F32) * pa + gated_b
    mixed = jnp.dot(merged.astype(BF16), wo_ref[...], preferred_element_type=F32)
    out_ref[0] = h_ref[0] + g_ref[0, 0] * mixed


def _merge(o_groups, lse_groups, out_b, gates, wa, wb, wo, h, mod4, k_gate):
    n_b, s, d = h.shape
    tm = MERGE_TM
    a_specs = [pl.BlockSpec((None, dil, tm // dil, D_A_OUT), lambda b, i: (b, 0, i, 0))
               for _, dil in DIL_GROUPS]
    row_spec = pl.BlockSpec((1, tm, d), lambda b, i: (b, i, 0))
    full = lambda shape: pl.BlockSpec(shape, lambda b, i: (0, 0))
    return pl.pallas_call(
        _merge_kernel,
        grid=(n_b, s // tm),
        in_specs=a_specs + a_specs + [
            pl.BlockSpec((1, tm, D_QB), lambda b, i: (b, i, 0)),
            pl.BlockSpec((1, tm, d), lambda b, i: (b, i, 0)),
            pl.BlockSpec((1, tm, d), lambda b, i: (b, i, 1)),
            full((D_A_OUT, d)), full((D_QB, d)), full((d, d)),
            row_spec,
            pl.BlockSpec((1, 1, 1, d), lambda b, i: (b, k_gate, 0, 0)),
        ],
        out_specs=row_spec,
        out_shape=jax.ShapeDtypeStruct((n_b, s, d), F32),
        scratch_shapes=[pltpu.VMEM((2 * N_DIL_GROUPS, HEADS_PER_DIL, tm, HEAD_DIM), F32)],
        compiler_params=_params(("parallel", "arbitrary")),
        name="mix_merge",
    )(*o_groups, *lse_groups, out_b, gates, gates, wa, wb, wo, h, mod4)


def _rope_tables(s):
    rows = s // GRID_W
    half = HEAD_DIM // 2
    inv_freq = ROPE_THETA ** (-jnp.arange(0, half, 2, dtype=F32) / half)
    ang_row = jnp.arange(rows).astype(F32)[:, None] * inv_freq
    ang_col = jnp.arange(GRID_W).astype(F32)[:, None] * inv_freq

    def expand(fn):
        return jnp.concatenate([jnp.repeat(fn(ang_row), GRID_W, axis=0),
                                jnp.tile(fn(ang_col), (rows, 1))], axis=-1)

    cos, sin = expand(jnp.cos), expand(jnp.sin)
    return jnp.concatenate([cos, cos], axis=-1), jnp.concatenate([-sin, sin], axis=-1)


def kernel(x, c, w_ada, b_ada, norm_ffn1, w1_ffn1, w3_ffn1, w2_ffn1, norm_mix, w_in, q_norm_a, k_norm_a, q_norm_b, k_norm_b, w_branch_a, w_branch_b, w_out, norm_ffn2, w1_ffn2, w3_ffn2, w2_ffn2, norm_final):
    n_b, s, d = x.shape
    assert w_ada.shape[0] == 1, "the fused pipeline covers a single layer"
    cos_f, sin_f = _rope_tables(s)
    h = x
    for l in range(1):
        mod4 = _ada(c, w_ada[l], b_ada[l]).reshape(n_b, N_MOD, 1, d)
        h, u, wi, wa, wb, wo = _ffn(
            h, mod4, norm_ffn1[l], w1_ffn1[l].astype(BF16), w3_ffn1[l].astype(BF16),
            w2_ffn1[l].astype(BF16), (0, 1, 2), "mid", norm_mix[l], (3, 4),
            to_cast=[w_in[l], w_branch_a[l], w_branch_b[l], w_out[l]])
        o_qb = 3 * D_A
        o_kb = o_qb + D_QB
        o_vb = o_kb + D_KVB
        o_gates = o_vb + D_KVB
        gain_qb = jnp.tile(q_norm_b[l], N_Q_B)[None]
        gain_kb = jnp.tile(k_norm_b[l], N_KV_B)[None]
        qa = _proj_classes(u, wi, 0, q_norm_a[l], scale=SM_SCALE)
        ka = _proj_classes(u, wi, 1, k_norm_a[l])
        va = _proj_classes(u, wi, 2)
        qb = _proj(u, wi, o_qb, D_QB, "norm_rope", 512, gain_qb, cos_f, sin_f, scale=SM_SCALE * LOG2E)
        kb = _proj(u, wi, o_kb, D_KVB, "norm_rope", 256, gain_kb, cos_f, sin_f)
        vb = _proj(u, wi, o_vb, D_KVB, "chunk_t", 256)
        gates = _proj(u, wi, o_gates, 2 * D_MODEL, "sigmoid", 2048)

        o_groups, lse_groups = [], []
        for g, (_, dil) in enumerate(DIL_GROUPS):
            o_g, lse_g = _dilated_group(qa[g], ka[g], va[g], g, dil)
            o_groups.append(o_g)
            lse_groups.append(lse_g)
        out_b, ffn2_w = _gqa(qb, kb, vb, [w1_ffn2[l], w3_ffn2[l], w2_ffn2[l]])
        h = _merge(o_groups, lse_groups, out_b, gates, wa, wb, wo, h, mod4, 5)
        (h,) = _ffn(h, mod4, norm_ffn2[l], *ffn2_w, (6, 7, 8), "final", norm_final)
    return h
```

```python
import functools

import jax
import jax.numpy as jnp
from jax import lax
from jax.experimental import pallas as pl
from jax.experimental.pallas import tpu as pltpu

F32 = jnp.float32
BF16 = jnp.bfloat16

D_MODEL = 2048
D_FF = 5632
HEAD_DIM = 128
DIL_GROUPS = ((128, 1), (512, 4), (2048, 16))
N_DIL_GROUPS = 3
HEADS_PER_DIL = 4
N_HEADS_A = N_DIL_GROUPS * HEADS_PER_DIL
N_Q_B = 8
N_KV_B = 2
Q_PER_KV = N_Q_B // N_KV_B
GRID_W = 64
ROPE_THETA = 10000.0
ALIBI_MAX = 8.0
N_MOD = 9
EPS = 1e-6
D_A = N_HEADS_A * HEAD_DIM
D_A_OUT = HEADS_PER_DIL * HEAD_DIM
D_QB = N_Q_B * HEAD_DIM
D_KVB = N_KV_B * HEAD_DIM
SM_SCALE = HEAD_DIM ** -0.5
MASK_VALUE = -1e30

LANES = 128
BF16_SUBLANES = 16
VMEM_LIMIT = 56 * 1024 * 1024


def _params(sem):
    return pltpu.CompilerParams(dimension_semantics=sem, vmem_limit_bytes=VMEM_LIMIT)


def _silu(x):
    return x * jax.nn.sigmoid(x)


def _rms(x, gain):
    ms = jnp.mean(x * x, axis=-1, keepdims=True)
    return x * lax.rsqrt(ms + EPS) * gain


def _rms_head(x, gain):
    sq = x * x
    hi = sq.astype(BF16)
    lo = (sq - hi.astype(F32)).astype(BF16)
    ones = jnp.ones((HEAD_DIM, HEAD_DIM), BF16)
    ssq = (jnp.dot(hi, ones, preferred_element_type=F32) + jnp.dot(lo, ones, preferred_element_type=F32))
    return x * lax.rsqrt(ssq * (1.0 / HEAD_DIM) + EPS) * gain


ADA_TN = 2048


def _ada_kernel(c_ref, w_ref, b_ref, o_ref):
    n_b = c_ref.shape[0]
    for b in range(n_b):
        cs = _silu(c_ref[b])
        for j in range(ADA_TN // LANES):
            sl = slice(j * LANES, (j + 1) * LANES)
            acc = jnp.sum(w_ref[:, sl] * cs, axis=0, keepdims=True)
            o_ref[b, :, sl] = acc + b_ref[:, sl]


def _ada(c, w, b):
    n_b, d = c.shape
    n = w.shape[1]
    c_b = jnp.broadcast_to(c[:, :, None], (n_b, d, LANES))
    return pl.pallas_call(
        _ada_kernel,
        grid=(n // ADA_TN,),
        in_specs=[
            pl.BlockSpec((n_b, d, LANES), lambda j: (0, 0, 0)),
            pl.BlockSpec((d, ADA_TN), lambda j: (0, j)),
            pl.BlockSpec((1, ADA_TN), lambda j: (0, j)),
        ],
        out_specs=pl.BlockSpec((n_b, 1, ADA_TN), lambda j: (0, 0, j)),
        out_shape=jax.ShapeDtypeStruct((n_b, 1, n), F32),
        compiler_params=_params(("arbitrary",)),
        name="ada_mod",
    )(c_b, w, b.reshape(1, n))


FFN_TM = 512
FFN_TF = 512
FFN_FIRST_ROWS = (256, 256)
FFN_LAST_ROWS = (256, 256)


def _ffn_kernel(x_ref, gain_ref, sh_ref, sc_ref, g_ref, w1_ref, w3_ref, w2_ref, *rest, mode, n_cast):
    if mode == "mid":
        gain2_ref, sh2_ref, sc2_ref = rest[:3]
        f32_refs, (h_ref, u2_ref) = rest[3:3 + n_cast], rest[3 + n_cast:5 + n_cast]
        bf16_refs, (u_scr, acc_scr) = rest[5 + n_cast:5 + 2 * n_cast], rest[5 + 2 * n_cast:]
    else:
        gainf_ref = rest[0]
        f32_refs, y_ref = rest[1:1 + n_cast], rest[1 + n_cast]
        bf16_refs, (u_scr, acc_scr) = rest[2 + n_cast:2 + 2 * n_cast], rest[2 + 2 * n_cast:]
    for src, dst in zip(f32_refs, bf16_refs):
        dst[...] = src[...].astype(BF16)
    f = pl.program_id(2)
    nf = pl.num_programs(2)
    def chunks(sizes):
        starts = [sum(sizes[:k]) for k in range(len(sizes))]
        return [slice(r, r + n) for r, n in zip(starts, sizes)]

    def swiglu_part(u):
        a = jnp.dot(u, w1_ref[...], preferred_element_type=F32)
        b = jnp.dot(u, w3_ref[...], preferred_element_type=F32)
        hm = (_silu(a) * b).astype(BF16)
        return jnp.dot(hm, w2_ref[...], preferred_element_type=F32)

    @pl.when(f == 0)
    def _():
        gain_mod = gain_ref[...] * (1.0 + sc_ref[0, 0])
        for rows in chunks(FFN_FIRST_ROWS):
            u = (_rms(x_ref[0, rows], gain_mod) + sh_ref[0, 0]).astype(BF16)
            u_scr[rows] = u
            acc_scr[rows] = swiglu_part(u)

    @pl.when((f > 0) & (f < nf - 1))
    def _():
        acc_scr[...] += swiglu_part(u_scr[...])

    @pl.when(f == nf - 1)
    def _():
        if mode == "mid":
            gain2_mod = gain2_ref[...] * (1.0 + sc2_ref[0, 0])
        for rows in chunks(FFN_LAST_ROWS):
            total = acc_scr[rows] + swiglu_part(u_scr[rows])
            h = x_ref[0, rows] + (0.5 * g_ref[0, 0]) * total
            if mode == "mid":
                h_ref[0, rows] = h
                u2_ref[0, rows] = (_rms(h, gain2_mod) + sh2_ref[0, 0]).astype(BF16)
            else:
                y_ref[0, rows] = _rms(h, gainf_ref[...])


def _mod_spec(k):
    return pl.BlockSpec((1, 1, 1, D_MODEL), lambda b, i, f, k=k: (b, k, 0, 0))


def _ffn_cast_spec(shape, n_b, n_i, n_f):
    rows = shape[0] // (n_b * n_i)
    assert rows * n_b * n_i == shape[0] and rows % BF16_SUBLANES == 0
    lane_blocks = shape[1] // LANES
    n_col = max(k for k in range(1, n_f + 1) if lane_blocks % k == 0)
    return pl.BlockSpec((rows, shape[1] // n_col),
                        lambda b, i, f: (b * n_i + i, jnp.minimum(f, n_col - 1)))


def _ffn(x, mod4, gain, w1, w3, w2, ks, mode, gain_next, ks_next=None, to_cast=()):
    n_b, s, d = x.shape
    tm, tf = FFN_TM, FFN_TF
    grid = (n_b, s // tm, D_FF // tf)
    cast_specs = [_ffn_cast_spec(arr.shape, *grid) for arr in to_cast]
    cast_shapes = [jax.ShapeDtypeStruct(arr.shape, BF16) for arr in to_cast]
    row_spec = pl.BlockSpec((1, tm, d), lambda b, i, f: (b, i, 0))
    vec_spec = pl.BlockSpec((1, d), lambda b, i, f: (0, 0))
    in_specs = [
        row_spec, vec_spec, _mod_spec(ks[0]), _mod_spec(ks[1]), _mod_spec(ks[2]),
        pl.BlockSpec((d, tf), lambda b, i, f: (0, f)),
        pl.BlockSpec((d, tf), lambda b, i, f: (0, f)),
        pl.BlockSpec((tf, d), lambda b, i, f: (f, 0)),
        vec_spec,
    ]
    args = [x, gain.reshape(1, d), mod4, mod4, mod4, w1, w3, w2, gain_next.reshape(1, d)]
    if mode == "mid":
        in_specs += [_mod_spec(ks_next[0]), _mod_spec(ks_next[1])]
        args += [mod4, mod4]
        out_specs = [row_spec, row_spec]
        out_shape = [jax.ShapeDtypeStruct((n_b, s, d), F32), jax.ShapeDtypeStruct((n_b, s, d), BF16)]
    else:
        out_specs = [row_spec]
        out_shape = [jax.ShapeDtypeStruct((n_b, s, d), F32)]
    return pl.pallas_call(
        functools.partial(_ffn_kernel, mode=mode, n_cast=len(to_cast)),
        grid=grid,
        in_specs=in_specs + cast_specs,
        out_specs=out_specs + cast_specs,
        out_shape=out_shape + cast_shapes,
        scratch_shapes=[pltpu.VMEM((tm, d), BF16), pltpu.VMEM((tm, d), F32)],
        compiler_params=_params(("arbitrary", "arbitrary", "arbitrary")),
        name="ffn_" + mode,
    )(*args, *to_cast)


PROJ_TM = 1024


def _proj_kernel(u_ref, w_ref, *rest, mode, tn, scale):
    if mode == "norm":
        gain_ref, o_ref = rest
    elif mode == "norm_rope":
        gain_ref, cos_ref, sin_ref, o_ref = rest
    else:
        (o_ref,) = rest
    acc = jnp.dot(u_ref[0], w_ref[...], preferred_element_type=F32)
    tm = acc.shape[0]
    if mode == "plain":
        o_ref[0] = acc.astype(BF16)
    elif mode == "chunk_t":
        acc_t = acc.T.astype(BF16)
        for hh in range(tn // HEAD_DIM):
            for cc in range(tm // GQA_TK):
                o_ref[0, hh, cc] = acc_t[hh * HEAD_DIM:(hh + 1) * HEAD_DIM, cc * GQA_TK:(cc + 1) * GQA_TK]
    elif mode == "sigmoid":
        o_ref[0] = jax.nn.sigmoid(acc).astype(BF16)
    else:
        for j in range(tn // HEAD_DIM):
            sl = slice(j * HEAD_DIM, (j + 1) * HEAD_DIM)
            y = _rms_head(acc[:, sl], gain_ref[:, sl])
            if mode == "norm_rope":
                y = y * cos_ref[...] + pltpu.roll(y, HEAD_DIM // 2, axis=1) * sin_ref[...]
            if scale != 1.0:
                y = y * scale
            o_ref[0, :, sl] = y.astype(BF16)


def _proj(u, w, col0, n, mode, tn, gain_row=None, cos=None, sin=None, scale=1.0):
    n_b, s, d = u.shape
    tm = PROJ_TM
    assert col0 % tn == 0 and n % tn == 0
    in_specs = [
        pl.BlockSpec((1, tm, d), lambda b, i, j: (b, i, 0)),
        pl.BlockSpec((d, tn), lambda b, i, j: (0, col0 // tn + j)),
    ]
    args = [u, w]
    if mode in ("norm", "norm_rope"):
        in_specs.append(pl.BlockSpec((1, tn), lambda b, i, j: (0, j)))
        args.append(gain_row)
    if mode == "norm_rope":
        in_specs += [pl.BlockSpec((tm, HEAD_DIM), lambda b, i, j: (i, 0))] * 2
        args += [cos, sin]
    if mode == "chunk_t":
        assert n == tn
        heads = n // HEAD_DIM
        out_spec = pl.BlockSpec((1, heads, tm // GQA_TK, HEAD_DIM, GQA_TK), lambda b, i, j: (b, 0, i, 0, 0))
        out_shape = jax.ShapeDtypeStruct((n_b, heads, s // GQA_TK, HEAD_DIM, GQA_TK), BF16)
    else:
        out_spec = pl.BlockSpec((1, tm, tn), lambda b, i, j: (b, i, j))
        out_shape = jax.ShapeDtypeStruct((n_b, s, n), BF16)
    return pl.pallas_call(
        functools.partial(_proj_kernel, mode=mode, tn=tn, scale=scale),
        grid=(n_b, s // tm, n // tn),
        in_specs=in_specs,
        out_specs=out_spec,
        out_shape=out_shape,
        compiler_params=_params(("parallel", "parallel", "arbitrary")),
        name="proj_" + mode,
    )(*args)


def _proj_classes_kernel(u_ref, w_ref, *rest, norm, scale):
    if norm:
        gain_ref, rest = rest[0], rest[1:]
    o_refs, scr = rest[:N_DIL_GROUPS], rest[N_DIL_GROUPS]
    acc = jnp.dot(u_ref[0], w_ref[...], preferred_element_type=F32)
    tm = acc.shape[0]
    for g, (_, dil) in enumerate(DIL_GROUPS):
        for h in range(HEADS_PER_DIL):
            head = g * HEADS_PER_DIL + h
            lanes = slice(h * HEAD_DIM, (h + 1) * HEAD_DIM)
            y = acc[:, head * HEAD_DIM:(head + 1) * HEAD_DIM]
            if norm:
                y = _rms_head(y, gain_ref[...])
            if scale != 1.0:
                y = y * scale
            if dil == 1:
                o_refs[g][0, 0, :, lanes] = y.astype(BF16)
                continue
            scr[head] = y
            for r in range(dil):
                o_refs[g][0, r, :, lanes] = scr[head, pl.ds(r, tm // dil, stride=dil), :].astype(BF16)


def _proj_classes(u, w, which, gain=None, scale=1.0):
    n_b, s, d = u.shape
    tm = PROJ_TM
    norm = gain is not None
    in_specs = [
        pl.BlockSpec((1, tm, d), lambda b, i: (b, i, 0)),
        pl.BlockSpec((d, D_A), lambda b, i: (0, which)),
    ]
    args = [u, w]
    if norm:
        in_specs.append(pl.BlockSpec((1, HEAD_DIM), lambda b, i: (0, 0)))
        args.append(gain.reshape(1, HEAD_DIM))
    return pl.pallas_call(
        functools.partial(_proj_classes_kernel, norm=norm, scale=scale),
        grid=(n_b, s // tm),
        in_specs=in_specs,
        out_specs=[pl.BlockSpec((1, dil, tm // dil, D_A_OUT), lambda b, i: (b, 0, i, 0))
                   for _, dil in DIL_GROUPS],
        out_shape=[jax.ShapeDtypeStruct((n_b, dil, s // dil, D_A_OUT), BF16) for _, dil in DIL_GROUPS],
        scratch_shapes=[pltpu.VMEM((N_HEADS_A, tm, HEAD_DIM), F32)],
        compiler_params=_params(("parallel", "arbitrary")),
        name="proj_classes",
    )(*args)


GQA_TQ = 512
GQA_TK = 512
LOG2E = 1.4426950408889634


def _gqa_kernel(q_ref, qn_ref, k_ref, vt_ref, *rest, n_cast):
    f32_refs, o_ref = rest[:n_cast], rest[n_cast]
    bf16_refs = rest[n_cast + 1:2 * n_cast + 1]
    acc_ref, s_ref = rest[2 * n_cast + 1:]
    for src, dst in zip(f32_refs, bf16_refs):
        dst[...] = src[...].astype(BF16)
    n_chunks = vt_ref.shape[2]
    tq = q_ref.shape[1]
    heads = range(Q_PER_KV)
    nt_dims = (((1,), (1,)), ((), ()))

    def k_chunk(c):
        return k_ref[0, pl.ds(pl.multiple_of(c * GQA_TK, GQA_TK), GQA_TK), :]

    def q_head(g, ref=q_ref):
        return ref[0, :, g * HEAD_DIM:(g + 1) * HEAD_DIM]

    ones_rows = jnp.ones((BF16_SUBLANES, GQA_TK), BF16)

    def step(c, c_next, slot, ms, next_q_ref=q_ref):
        k_next = k_chunk(c_next)
        vt = jnp.concatenate([vt_ref[0, 0, c], ones_rows], axis=0)
        new_ms = []
        for g in heads:
            s_ref[1 - slot, g] = lax.dot_general(k_next, q_head(g, next_q_ref), nt_dims,
                                                 preferred_element_type=F32)
            m_new = jnp.maximum(ms[g], jnp.max(s_ref[slot, g], axis=0, keepdims=True))
            alpha = jnp.exp2(ms[g] - m_new)
            p = jnp.exp2(s_ref[slot, g] - m_new).astype(BF16)
            new_ms.append(m_new)
            acc_ref[g] = alpha * acc_ref[g] + jnp.dot(vt, p, preferred_element_type=F32)
        return tuple(new_ms)

    acc_ref[...] = jnp.zeros(acc_ref.shape, F32)

    @pl.when(pl.program_id(2) == 0)
    def _():
        k0 = k_chunk(0)
        for g in heads:
            s_ref[0, g] = lax.dot_general(k0, q_head(g), nt_dims, preferred_element_type=F32)

    def body(i, ms):
        ms = step(2 * i, 2 * i + 1, 0, ms)
        return step(2 * i + 1, 2 * i + 2, 1, ms)

    init = tuple(jnp.full((1, tq), MASK_VALUE, F32) for _ in heads)
    ms = lax.fori_loop(0, n_chunks // 2 - 1, body, init)
    ms = step(n_chunks - 2, n_chunks - 1, 0, ms)
    step(n_chunks - 1, 0, 1, ms, qn_ref)
    for g in heads:
        o_t = acc_ref[g, :HEAD_DIM] / acc_ref[g, HEAD_DIM:HEAD_DIM + 1]
        o_ref[0, :, g * HEAD_DIM:(g + 1) * HEAD_DIM] = o_t.T.astype(BF16)


def _cast_block(n_rows, n_steps):
    per = 1
    while n_rows % (n_steps // per) or (n_rows // (n_steps // per)) % BF16_SUBLANES:
        per *= 2
        assert per <= n_steps
    return n_rows // (n_steps // per), per


def _gqa(qb, kb, vbt, to_cast):
    n_b, s, _ = qb.shape
    tq = GQA_TQ
    w = Q_PER_KV * HEAD_DIM
    nq = s // tq
    n_steps = n_b * N_KV_B * nq
    cast_specs = []
    for arr in to_cast:
        rows, per = _cast_block(arr.shape[0], n_steps)
        cast_specs.append(pl.BlockSpec(
            (rows, arr.shape[1]), lambda b, h, i, per=per: (((b * N_KV_B + h) * nq + i) // per, 0)))
    out = pl.pallas_call(
        functools.partial(_gqa_kernel, n_cast=len(to_cast)),
        grid=(n_b, N_KV_B, nq),
        in_specs=[
            pl.BlockSpec((1, tq, w), lambda b, h, i: (b, i, h)),
            pl.BlockSpec((1, tq, w), lambda b, h, i: (b, jnp.minimum(i + 1, nq - 1), h)),
            pl.BlockSpec((1, s, HEAD_DIM), lambda b, h, i: (b, 0, h)),
            pl.BlockSpec((1, 1, s // GQA_TK, HEAD_DIM, GQA_TK), lambda b, h, i: (b, h, 0, 0, 0)),
        ] + cast_specs,
        out_specs=[pl.BlockSpec((1, tq, w), lambda b, h, i: (b, i, h))] + cast_specs,
        out_shape=[jax.ShapeDtypeStruct((n_b, s, D_QB), BF16)]
        + [jax.ShapeDtypeStruct(arr.shape, BF16) for arr in to_cast],
        scratch_shapes=[pltpu.VMEM((Q_PER_KV, HEAD_DIM + BF16_SUBLANES, tq), F32),
                        pltpu.VMEM((2, Q_PER_KV, GQA_TK, tq), F32)],
        compiler_params=_params(("arbitrary", "arbitrary", "arbitrary")),
        name="gqa_attn",
    )(qb, qb, kb, vbt, *to_cast)
    return out[0], out[1:]


DIL_TQ = 512
DIL_SUB = 128
DIL_HALF = 64


def _dil_kernel(q_ref, kp_ref, kc_ref, kn_ref, vp_ref, vc_ref, vn_ref, o_ref, lse_ref, *, group, dil, n_rows):
    i = pl.program_id(2)
    win = DIL_SUB + 2 * DIL_HALF
    row = lax.broadcasted_iota(jnp.int32, (DIL_SUB, win), 0)
    col = lax.broadcasted_iota(jnp.int32, (DIL_SUB, win), 1)
    dist = jnp.abs(col - DIL_HALF - row)
    in_band = dist <= DIL_HALF
    dist_f = dist.astype(F32) * float(dil)
    for h in range(HEADS_PER_DIL):
        sl = slice(h * HEAD_DIM, (h + 1) * HEAD_DIM)
        slope = 2.0 ** (-ALIBI_MAX * (group * HEADS_PER_DIL + h + 1) / N_HEADS_A)
        k = jnp.concatenate([kp_ref[:, sl], kc_ref[:, sl], kn_ref[:, sl]], axis=0)
        v = jnp.concatenate([vp_ref[:, sl], vc_ref[:, sl], vn_ref[:, sl]], axis=0)
        for j in range(DIL_TQ // DIL_SUB):
            rows = slice(j * DIL_SUB, (j + 1) * DIL_SUB)
            k_abs = i * DIL_TQ + (j * DIL_SUB - DIL_HALF) + col
            valid = in_band & (k_abs >= 0) & (k_abs < n_rows)
            s = lax.dot_general(q_ref[rows, sl], k[j * DIL_SUB:j * DIL_SUB + win],
                                (((1,), (1,)), ((), ())), preferred_element_type=F32)
            s = jnp.where(valid, s - slope * dist_f, MASK_VALUE)
            m = jnp.max(s, axis=-1, keepdims=True)
            p = jnp.exp(s - m)
            l = jnp.sum(p, axis=-1, keepdims=True)
            o = jnp.dot(p.astype(BF16), v[j * DIL_SUB:j * DIL_SUB + win], preferred_element_type=F32)
            o_ref[rows, sl] = (o / l).astype(BF16)
            lse_ref[rows, sl] = jnp.broadcast_to(m + jnp.log(l), (DIL_SUB, HEAD_DIM))


def _dilated_group(q, k, v, group, dil):
    n_b, _, n_rows, w = q.shape
    tq = DIL_TQ
    halo_per_blk = tq // DIL_HALF
    n_halo = n_rows // DIL_HALF
    cur = pl.BlockSpec((None, None, tq, w), lambda b, r, i: (b, r, i, 0))
    prev = pl.BlockSpec((None, None, DIL_HALF, w),
                        lambda b, r, i: (b, r, jnp.maximum(i * halo_per_blk - 1, 0), 0))
    nxt = pl.BlockSpec((None, None, DIL_HALF, w),
                       lambda b, r, i: (b, r, jnp.minimum((i + 1) * halo_per_blk, n_halo - 1), 0))
    return pl.pallas_call(
        functools.partial(_dil_kernel, group=group, dil=dil, n_rows=n_rows),
        grid=(n_b, dil, n_rows // tq),
        in_specs=[cur, prev, cur, nxt, prev, cur, nxt],
        out_specs=[cur, cur],
        out_shape=[jax.ShapeDtypeStruct((n_b, dil, n_rows, w), BF16),
                   jax.ShapeDtypeStruct((n_b, dil, n_rows, w), F32)],
        compiler_params=_params(("parallel", "parallel", "arbitrary")),
        name="dilated_attn_g%d" % group,
    )(q, k, k, k, v, v, v)


MERGE_TM = 256


def _merge_kernel(o0_ref, o1_ref, o2_ref, l0_ref, l1_ref, l2_ref, ob_ref, sa_ref, sb_ref,
                  wa_ref, wb_ref, wo_ref, h_ref, g_ref, out_ref, nat_ref):
    tm = out_ref.shape[1]

    def natural(ref, dil, slot):
        if dil == 1:
            return ref[0].astype(F32)
        for r in range(dil):
            for h in range(HEADS_PER_DIL):
                nat_ref[slot, h, pl.ds(r, tm // dil, stride=dil), :] = (
                    ref[r, :, h * HEAD_DIM:(h + 1) * HEAD_DIM].astype(F32))
        return jnp.concatenate([nat_ref[slot, h] for h in range(HEADS_PER_DIL)], axis=1)

    gated_b = sb_ref[0].astype(F32) * jnp.dot(ob_ref[0], wb_ref[...], preferred_element_type=F32)
    dils = [dil for _, dil in DIL_GROUPS]
    l0, l1, l2 = [natural(ref, dil, n) for n, (ref, dil) in enumerate(zip((l0_ref, l1_ref, l2_ref), dils))]
    o0, o1, o2 = [natural(ref, dil, N_DIL_GROUPS + n)
                  for n, (ref, dil) in enumerate(zip((o0_ref, o1_ref, o2_ref), dils))]
    mx = jnp.maximum(jnp.maximum(l0, l1), l2)
    e0, e1, e2 = jnp.exp(l0 - mx), jnp.exp(l1 - mx), jnp.exp(l2 - mx)
    oa = (e0 * o0 + e1 * o1 + e2 * o2) / (e0 + e1 + e2)
    pa = jnp.dot(oa.astype(BF16), wa_ref[...], preferred_element_type=F32)
    merged = sa_ref[0].astype(F32) * pa + gated_b
    mixed = jnp.dot(merged.astype(BF16), wo_ref[...], preferred_element_type=F32)
    out_ref[0] = h_ref[0] + g_ref[0, 0] * mixed


def _merge(o_groups, lse_groups, out_b, gates, wa, wb, wo, h, mod4, k_gate):
    n_b, s, d = h.shape
    tm = MERGE_TM
    a_specs = [pl.BlockSpec((None, dil, tm // dil, D_A_OUT), lambda b, i: (b, 0, i, 0))
               for _, dil in DIL_GROUPS]
    row_spec = pl.BlockSpec((1, tm, d), lambda b, i: (b, i, 0))
    full = lambda shape: pl.BlockSpec(shape, lambda b, i: (0, 0))
    return pl.pallas_call(
        _merge_kernel,
        grid=(n_b, s // tm),
        in_specs=a_specs + a_specs + [
            pl.BlockSpec((1, tm, D_QB), lambda b, i: (b, i, 0)),
            pl.BlockSpec((1, tm, d), lambda b, i: (b, i, 0)),
            pl.BlockSpec((1, tm, d), lambda b, i: (b, i, 1)),
            full((D_A_OUT, d)), full((D_QB, d)), full((d, d)),
            row_spec,
            pl.BlockSpec((1, 1, 1, d), lambda b, i: (b, k_gate, 0, 0)),
        ],
        out_specs=row_spec,
        out_shape=jax.ShapeDtypeStruct((n_b, s, d), F32),
        scratch_shapes=[pltpu.VMEM((2 * N_DIL_GROUPS, HEADS_PER_DIL, tm, HEAD_DIM), F32)],
        compiler_params=_params(("parallel", "arbitrary")),
        name="mix_merge",
    )(*o_groups, *lse_groups, out_b, gates, gates, wa, wb, wo, h, mod4)


def _rope_tables(s):
    rows = s // GRID_W
    half = HEAD_DIM // 2
    inv_freq = ROPE_THETA ** (-jnp.arange(0, half, 2, dtype=F32) / half)
    ang_row = jnp.arange(rows).astype(F32)[:, None] * inv_freq
    ang_col = jnp.arange(GRID_W).astype(F32)[:, None] * inv_freq

    def expand(fn):
        return jnp.concatenate([jnp.repeat(fn(ang_row), GRID_W, axis=0),
                                jnp.tile(fn(ang_col), (rows, 1))], axis=-1)

    cos, sin = expand(jnp.cos), expand(jnp.sin)
    return jnp.concatenate([cos, cos], axis=-1), jnp.concatenate([-sin, sin], axis=-1)


def kernel(x, c, w_ada, b_ada, norm_ffn1, w1_ffn1, w3_ffn1, w2_ffn1, norm_mix, w_in, q_norm_a, k_norm_a, q_norm_b, k_norm_b, w_branch_a, w_branch_b, w_out, norm_ffn2, w1_ffn2, w3_ffn2, w2_ffn2, norm_final):
    n_b, s, d = x.shape
    assert w_ada.shape[0] == 1, "the fused pipeline covers a single layer"
    cos_f, sin_f = _rope_tables(s)
    h = x
    for l in range(1):
        mod4 = _ada(c, w_ada[l], b_ada[l]).reshape(n_b, N_MOD, 1, d)
        h, u, wi, wa, wb, wo = _ffn(
            h, mod4, norm_ffn1[l], w1_ffn1[l].astype(BF16), w3_ffn1[l].astype(BF16),
            w2_ffn1[l].astype(BF16), (0, 1, 2), "mid", norm_mix[l], (3, 4),
            to_cast=[w_in[l], w_branch_a[l], w_branch_b[l], w_out[l]])
        o_qb = 3 * D_A
        o_kb = o_qb + D_QB
        o_vb = o_kb + D_KVB
        o_gates = o_vb + D_KVB
        gain_qb = jnp.tile(q_norm_b[l], N_Q_B)[None]
        gain_kb = jnp.tile(k_norm_b[l], N_KV_B)[None]
        qa = _proj_classes(u, wi, 0, q_norm_a[l], scale=SM_SCALE)
        ka = _proj_classes(u, wi, 1, k_norm_a[l])
        va = _proj_classes(u, wi, 2)
        qb = _proj(u, wi, o_qb, D_QB, "norm_rope", 512, gain_qb, cos_f, sin_f, scale=SM_SCALE * LOG2E)
        kb = _proj(u, wi, o_kb, D_KVB, "norm_rope", 256, gain_kb, cos_f, sin_f)
        vb = _proj(u, wi, o_vb, D_KVB, "chunk_t", 256)
        gates = _proj(u, wi, o_gates, 2 * D_MODEL, "sigmoid", 2048)

        o_groups, lse_groups = [], []
        for g, (_, dil) in enumerate(DIL_GROUPS):
            o_g, lse_g = _dilated_group(qa[g], ka[g], va[g], g, dil)
            o_groups.append(o_g)
            lse_groups.append(lse_g)
        out_b, ffn2_w = _gqa(qb, kb, vb, [w1_ffn2[l], w3_ffn2[l], w2_ffn2[l]])
        h = _merge(o_groups, lse_groups, out_b, gates, wa, wb, wo, h, mod4, 5)
        (h,) = _ffn(h, mod4, norm_ffn2[l], *ffn2_w, (6, 7, 8), "final", norm_final)
    return h
```

```python
import functools

import jax
import jax.numpy as jnp
from jax import lax
from jax.experimental import pallas as pl
from jax.experimental.pallas import tpu as pltpu

F32 = jnp.float32
BF16 = jnp.bfloat16

D_MODEL = 2048
D_FF = 5632
HEAD_DIM = 128
DIL_GROUPS = ((128, 1), (512, 4), (2048, 16))
N_DIL_GROUPS = 3
HEADS_PER_DIL = 4
N_HEADS_A = N_DIL_GROUPS * HEADS_PER_DIL
N_Q_B = 8
N_KV_B = 2
Q_PER_KV = N_Q_B // N_KV_B
GRID_W = 64
ROPE_THETA = 10000.0
ALIBI_MAX = 8.0
N_MOD = 9
EPS = 1e-6
D_A = N_HEADS_A * HEAD_DIM
D_A_OUT = HEADS_PER_DIL * HEAD_DIM
D_QB = N_Q_B * HEAD_DIM
D_KVB = N_KV_B * HEAD_DIM
SM_SCALE = HEAD_DIM ** -0.5
MASK_VALUE = -1e30

LANES = 128
BF16_SUBLANES = 16
VMEM_LIMIT = 56 * 1024 * 1024


def _params(sem):
    return pltpu.CompilerParams(dimension_semantics=sem, vmem_limit_bytes=VMEM_LIMIT)


def _silu(x):
    return x * jax.nn.sigmoid(x)


def _rms(x, gain):
    ms = jnp.mean(x * x, axis=-1, keepdims=True)
    return x * lax.rsqrt(ms + EPS) * gain


def _rms_head(x, gain):
    sq = x * x
    hi = sq.astype(BF16)
    lo = (sq - hi.astype(F32)).astype(BF16)
    ones = jnp.ones((HEAD_DIM, HEAD_DIM), BF16)
    ssq = (jnp.dot(hi, ones, preferred_element_type=F32) + jnp.dot(lo, ones, preferred_element_type=F32))
    return x * lax.rsqrt(ssq * (1.0 / HEAD_DIM) + EPS) * gain


ADA_TN = 2048


def _ada_kernel(c_ref, w_ref, b_ref, o_ref):
    n_b = c_ref.shape[0]
    for b in range(n_b):
        cs = _silu(c_ref[b])
        for j in range(ADA_TN // LANES):
            sl = slice(j * LANES, (j + 1) * LANES)
            acc = jnp.sum(w_ref[:, sl] * cs, axis=0, keepdims=True)
            o_ref[b, :, sl] = acc + b_ref[:, sl]


def _ada(c, w, b):
    n_b, d = c.shape
    n = w.shape[1]
    c_b = jnp.broadcast_to(c[:, :, None], (n_b, d, LANES))
    return pl.pallas_call(
        _ada_kernel,
        grid=(n // ADA_TN,),
        in_specs=[
            pl.BlockSpec((n_b, d, LANES), lambda j: (0, 0, 0)),
            pl.BlockSpec((d, ADA_TN), lambda j: (0, j)),
            pl.BlockSpec((1, ADA_TN), lambda j: (0, j)),
        ],
        out_specs=pl.BlockSpec((n_b, 1, ADA_TN), lambda j: (0, 0, j)),
        out_shape=jax.ShapeDtypeStruct((n_b, 1, n), F32),
        compiler_params=_params(("arbitrary",)),
        name="ada_mod",
    )(c_b, w, b.reshape(1, n))


FFN_TM = 512
FFN_TF = 512
FFN_FIRST_ROWS = (256, 256)
FFN_LAST_ROWS = (256, 256)


def _ffn_kernel(x_ref, gain_ref, sh_ref, sc_ref, g_ref, w1_ref, w3_ref, w2_ref, *rest, mode, n_cast):
    if mode == "mid":
        gain2_ref, sh2_ref, sc2_ref = rest[:3]
        f32_refs, (h_ref, u2_ref) = rest[3:3 + n_cast], rest[3 + n_cast:5 + n_cast]
        bf16_refs, (u_scr, acc_scr) = rest[5 + n_cast:5 + 2 * n_cast], rest[5 + 2 * n_cast:]
    else:
        gainf_ref = rest[0]
        f32_refs, y_ref = rest[1:1 + n_cast], rest[1 + n_cast]
        bf16_refs, (u_scr, acc_scr) = rest[2 + n_cast:2 + 2 * n_cast], rest[2 + 2 * n_cast:]
    for src, dst in zip(f32_refs, bf16_refs):
        dst[...] = src[...].astype(BF16)
    f = pl.program_id(2)
    nf = pl.num_programs(2)
    def chunks(sizes):
        starts = [sum(sizes[:k]) for k in range(len(sizes))]
        return [slice(r, r + n) for r, n in zip(starts, sizes)]

    def swiglu_part(u):
        a = jnp.dot(u, w1_ref[...], preferred_element_type=F32)
        b = jnp.dot(u, w3_ref[...], preferred_element_type=F32)
        hm = (_silu(a) * b).astype(BF16)
        return jnp.dot(hm, w2_ref[...], preferred_element_type=F32)

    @pl.when(f == 0)
    def _():
        gain_mod = gain_ref[...] * (1.0 + sc_ref[0, 0])
        for rows in chunks(FFN_FIRST_ROWS):
            u = (_rms(x_ref[0, rows], gain_mod) + sh_ref[0, 0]).astype(BF16)
            u_scr[rows] = u
            acc_scr[rows] = swiglu_part(u)

    @pl.when((f > 0) & (f < nf - 1))
    def _():
        acc_scr[...] += swiglu_part(u_scr[...])

    @pl.when(f == nf - 1)
    def _():
        if mode == "mid":
            gain2_mod = gain2_ref[...] * (1.0 + sc2_ref[0, 0])
        for rows in chunks(FFN_LAST_ROWS):
            total = acc_scr[rows] + swiglu_part(u_scr[rows])
            h = x_ref[0, rows] + (0.5 * g_ref[0, 0]) * total
            if mode == "mid":
                h_ref[0, rows] = h
                u2_ref[0, rows] = (_rms(h, gain2_mod) + sh2_ref[0, 0]).astype(BF16)
            else:
                y_ref[0, rows] = _rms(h, gainf_ref[...])


def _mod_spec(k):
    return pl.BlockSpec((1, 1, 1, D_MODEL), lambda b, i, f, k=k: (b, k, 0, 0))


def _ffn_cast_spec(shape, n_b, n_i, n_f):
    rows = shape[0] // (n_b * n_i)
    assert rows * n_b * n_i == shape[0] and rows % BF16_SUBLANES == 0
    lane_blocks = shape[1] // LANES
    n_col = max(k for k in range(1, n_f + 1) if lane_blocks % k == 0)
    return pl.BlockSpec((rows, shape[1] // n_col),
                        lambda b, i, f: (b * n_i + i, jnp.minimum(f, n_col - 1)))


def _ffn(x, mod4, gain, w1, w3, w2, ks, mode, gain_next, ks_next=None, to_cast=()):
    n_b, s, d = x.shape
    tm, tf = FFN_TM, FFN_TF
    grid = (n_b, s // tm, D_FF // tf)
    cast_specs = [_ffn_cast_spec(arr.shape, *grid) for arr in to_cast]
    cast_shapes = [jax.ShapeDtypeStruct(arr.shape, BF16) for arr in to_cast]
    row_spec = pl.BlockSpec((1, tm, d), lambda b, i, f: (b, i, 0))
    vec_spec = pl.BlockSpec((1, d), lambda b, i, f: (0, 0))
    in_specs = [
        row_spec, vec_spec, _mod_spec(ks[0]), _mod_spec(ks[1]), _mod_spec(ks[2]),
        pl.BlockSpec((d, tf), lambda b, i, f: (0, f)),
        pl.BlockSpec((d, tf), lambda b, i, f: (0, f)),
        pl.BlockSpec((tf, d), lambda b, i, f: (f, 0)),
        vec_spec,
    ]
    args = [x, gain.reshape(1, d), mod4, mod4, mod4, w1, w3, w2, gain_next.reshape(1, d)]
    if mode == "mid":
        in_specs += [_mod_spec(ks_next[0]), _mod_spec(ks_next[1])]
        args += [mod4, mod4]
        out_specs = [row_spec, row_spec]
        out_shape = [jax.ShapeDtypeStruct((n_b, s, d), F32), jax.ShapeDtypeStruct((n_b, s, d), BF16)]
    else:
        out_specs = [row_spec]
        out_shape = [jax.ShapeDtypeStruct((n_b, s, d), F32)]
    return pl.pallas_call(
        functools.partial(_ffn_kernel, mode=mode, n_cast=len(to_cast)),
        grid=grid,
        in_specs=in_specs + cast_specs,
        out_specs=out_specs + cast_specs,
        out_shape=out_shape + cast_shapes,
        scratch_shapes=[pltpu.VMEM((tm, d), BF16), pltpu.VMEM((tm, d), F32)],
        compiler_params=_params(("arbitrary", "arbitrary", "arbitrary")),
        name="ffn_" + mode,
    )(*args, *to_cast)


PROJ_TM = 1024


def _proj_kernel(u_ref, w_ref, *rest, mode, tn, scale):
    if mode == "norm":
        gain_ref, o_ref = rest
    elif mode == "norm_rope":
        gain_ref, cos_ref, sin_ref, o_ref = rest
    else:
        (o_ref,) = rest
    acc = jnp.dot(u_ref[0], w_ref[...], preferred_element_type=F32)
    tm = acc.shape[0]
    if mode == "plain":
        o_ref[0] = acc.astype(BF16)
    elif mode == "chunk_t":
        acc_t = acc.T.astype(BF16)
        for hh in range(tn // HEAD_DIM):
            for cc in range(tm // GQA_TK):
                o_ref[0, hh, cc] = acc_t[hh * HEAD_DIM:(hh + 1) * HEAD_DIM, cc * GQA_TK:(cc + 1) * GQA_TK]
    elif mode == "sigmoid":
        o_ref[0] = jax.nn.sigmoid(acc).astype(BF16)
    else:
        for j in range(tn // HEAD_DIM):
            sl = slice(j * HEAD_DIM, (j + 1) * HEAD_DIM)
            y = _rms_head(acc[:, sl], gain_ref[:, sl])
            if mode == "norm_rope":
                y = y * cos_ref[...] + pltpu.roll(y, HEAD_DIM // 2, axis=1) * sin_ref[...]
            if scale != 1.0:
                y = y * scale
            o_ref[0, :, sl] = y.astype(BF16)


def _proj(u, w, col0, n, mode, tn, gain_row=None, cos=None, sin=None, scale=1.0):
    n_b, s, d = u.shape
    tm = PROJ_TM
    assert col0 % tn == 0 and n % tn == 0
    in_specs = [
        pl.BlockSpec((1, tm, d), lambda b, i, j: (b, i, 0)),
        pl.BlockSpec((d, tn), lambda b, i, j: (0, col0 // tn + j)),
    ]
    args = [u, w]
    if mode in ("norm", "norm_rope"):
        in_specs.append(pl.BlockSpec((1, tn), lambda b, i, j: (0, j)))
        args.append(gain_row)
    if mode == "norm_rope":
        in_specs += [pl.BlockSpec((tm, HEAD_DIM), lambda b, i, j: (i, 0))] * 2
        args += [cos, sin]
    if mode == "chunk_t":
        assert n == tn
        heads = n // HEAD_DIM
        out_spec = pl.BlockSpec((1, heads, tm // GQA_TK, HEAD_DIM, GQA_TK), lambda b, i, j: (b, 0, i, 0, 0))
        out_shape = jax.ShapeDtypeStruct((n_b, heads, s // GQA_TK, HEAD_DIM, GQA_TK), BF16)
    else:
        out_spec = pl.BlockSpec((1, tm, tn), lambda b, i, j: (b, i, j))
        out_shape = jax.ShapeDtypeStruct((n_b, s, n), BF16)
    return pl.pallas_call(
        functools.partial(_proj_kernel, mode=mode, tn=tn, scale=scale),
        grid=(n_b, s // tm, n // tn),
        in_specs=in_specs,
        out_specs=out_spec,
        out_shape=out_shape,
        compiler_params=_params(("parallel", "parallel", "arbitrary")),
        name="proj_" + mode,
    )(*args)


def _proj_classes_kernel(u_ref, w_ref, *rest, norm, scale):
    if norm:
        gain_ref, rest = rest[0], rest[1:]
    o_refs, scr = rest[:N_DIL_GROUPS], rest[N_DIL_GROUPS]
    acc = jnp.dot(u_ref[0], w_ref[...], preferred_element_type=F32)
    tm = acc.shape[0]
    for g, (_, dil) in enumerate(DIL_GROUPS):
        for h in range(HEADS_PER_DIL):
            head = g * HEADS_PER_DIL + h
            lanes = slice(h * HEAD_DIM, (h + 1) * HEAD_DIM)
            y = acc[:, head * HEAD_DIM:(head + 1) * HEAD_DIM]
            if norm:
                y = _rms_head(y, gain_ref[...])
            if scale != 1.0:
                y = y * scale
            if dil == 1:
                o_refs[g][0, 0, :, lanes] = y.astype(BF16)
                continue
            scr[head] = y
            for r in range(dil):
                o_refs[g][0, r, :, lanes] = scr[head, pl.ds(r, tm // dil, stride=dil), :].astype(BF16)


def _proj_classes(u, w, which, gain=None, scale=1.0):
    n_b, s, d = u.shape
    tm = PROJ_TM
    norm = gain is not None
    in_specs = [
        pl.BlockSpec((1, tm, d), lambda b, i: (b, i, 0)),
        pl.BlockSpec((d, D_A), lambda b, i: (0, which)),
    ]
    args = [u, w]
    if norm:
        in_specs.append(pl.BlockSpec((1, HEAD_DIM), lambda b, i: (0, 0)))
        args.append(gain.reshape(1, HEAD_DIM))
    return pl.pallas_call(
        functools.partial(_proj_classes_kernel, norm=norm, scale=scale),
        grid=(n_b, s // tm),
        in_specs=in_specs,
        out_specs=[pl.BlockSpec((1, dil, tm // dil, D_A_OUT), lambda b, i: (b, 0, i, 0))
                   for _, dil in DIL_GROUPS],
        out_shape=[jax.ShapeDtypeStruct((n_b, dil, s // dil, D_A_OUT), BF16) for _, dil in DIL_GROUPS],
        scratch_shapes=[pltpu.VMEM((N_HEADS_A, tm, HEAD_DIM), F32)],
        compiler_params=_params(("parallel", "arbitrary")),
        name="proj_classes",
    )(*args)


GQA_TQ = 512
GQA_TK = 512
LOG2E = 1.4426950408889634


def _gqa_kernel(q_ref, qn_ref, k_ref, vt_ref, *rest, n_cast):
    f32_refs, o_ref = rest[:n_cast], rest[n_cast]
    bf16_refs = rest[n_cast + 1:2 * n_cast + 1]
    acc_ref, s_ref = rest[2 * n_cast + 1:]
    for src, dst in zip(f32_refs, bf16_refs):
        dst[...] = src[...].astype(BF16)
    n_chunks = vt_ref.shape[2]
    tq = q_ref.shape[1]
    heads = range(Q_PER_KV)
    nt_dims = (((1,), (1,)), ((), ()))

    def k_chunk(c):
        return k_ref[0, pl.ds(pl.multiple_of(c * GQA_TK, GQA_TK), GQA_TK), :]

    def q_head(g, ref=q_ref):
        return ref[0, :, g * HEAD_DIM:(g + 1) * HEAD_DIM]

    ones_rows = jnp.ones((BF16_SUBLANES, GQA_TK), BF16)

    def step(c, c_next, slot, ms, next_q_ref=q_ref):
        k_next = k_chunk(c_next)
        vt = jnp.concatenate([vt_ref[0, 0, c], ones_rows], axis=0)
        new_ms = []
        for g in heads:
            s_ref[1 - slot, g] = lax.dot_general(k_next, q_head(g, next_q_ref), nt_dims,
                                                 preferred_element_type=F32)
            m_new = jnp.maximum(ms[g], jnp.max(s_ref[slot, g], axis=0, keepdims=True))
            alpha = jnp.exp2(ms[g] - m_new)
            p = jnp.exp2(s_ref[slot, g] - m_new).astype(BF16)
            new_ms.append(m_new)
            acc_ref[g] = alpha * acc_ref[g] + jnp.dot(vt, p, preferred_element_type=F32)
        return tuple(new_ms)

    acc_ref[...] = jnp.zeros(acc_ref.shape, F32)

    @pl.when(pl.program_id(2) == 0)
    def _():
        k0 = k_chunk(0)
        for g in heads:
            s_ref[0, g] = lax.dot_general(k0, q_head(g), nt_dims, preferred_element_type=F32)

    def body(i, ms):
        ms = step(2 * i, 2 * i + 1, 0, ms)
        return step(2 * i + 1, 2 * i + 2, 1, ms)

    init = tuple(jnp.full((1, tq), MASK_VALUE, F32) for _ in heads)
    ms = lax.fori_loop(0, n_chunks // 2 - 1, body, init)
    ms = step(n_chunks - 2, n_chunks - 1, 0, ms)
    step(n_chunks - 1, 0, 1, ms, qn_ref)
    for g in heads:
        o_t = acc_ref[g, :HEAD_DIM] / acc_ref[g, HEAD_DIM:HEAD_DIM + 1]
        o_ref[0, :, g * HEAD_DIM:(g + 1) * HEAD_DIM] = o_t.T.astype(BF16)


def _cast_block(n_rows, n_steps):
    per = 1
    while n_rows % (n_steps // per) or (n_rows // (n_steps // per)) % BF16_SUBLANES:
        per *= 2
        assert per <= n_steps
    return n_rows // (n_steps // per), per


def _gqa(qb, kb, vbt, to_cast):
    n_b, s, _ = qb.shape
    tq = GQA_TQ
    w = Q_PER_KV * HEAD_DIM
    nq = s // tq
    n_steps = n_b * N_KV_B * nq
    cast_specs = []
    for arr in to_cast:
        rows, per = _cast_block(arr.shape[0], n_steps)
        cast_specs.append(pl.BlockSpec(
            (rows, arr.shape[1]), lambda b, h, i, per=per: (((b * N_KV_B + h) * nq + i) // per, 0)))
    out = pl.pallas_call(
        functools.partial(_gqa_kernel, n_cast=len(to_cast)),
        grid=(n_b, N_KV_B, nq),
        in_specs=[
            pl.BlockSpec((1, tq, w), lambda b, h, i: (b, i, h)),
            pl.BlockSpec((1, tq, w), lambda b, h, i: (b, jnp.minimum(i + 1, nq - 1), h)),
            pl.BlockSpec((1, s, HEAD_DIM), lambda b, h, i: (b, 0, h)),
            pl.BlockSpec((1, 1, s // GQA_TK, HEAD_DIM, GQA_TK), lambda b, h, i: (b, h, 0, 0, 0)),
        ] + cast_specs,
        out_specs=[pl.BlockSpec((1, tq, w), lambda b, h, i: (b, i, h))] + cast_specs,
        out_shape=[jax.ShapeDtypeStruct((n_b, s, D_QB), BF16)]
        + [jax.ShapeDtypeStruct(arr.shape, BF16) for arr in to_cast],
        scratch_shapes=[pltpu.VMEM((Q_PER_KV, HEAD_DIM + BF16_SUBLANES, tq), F32),
                        pltpu.VMEM((2, Q_PER_KV, GQA_TK, tq), F32)],
        compiler_params=_params(("arbitrary", "arbitrary", "arbitrary")),
        name="gqa_attn",
    )(qb, qb, kb, vbt, *to_cast)
    return out[0], out[1:]


DIL_TQ = 512
DIL_SUB = 128
DIL_HALF = 64


def _dil_tile(q_ref, kp_ref, kc_ref, kn_ref, vp_ref, vc_ref, vn_ref, o_ref, lse_ref, i, heads, *,
              group, dil, n_rows):
    win = DIL_SUB + 2 * DIL_HALF
    row = lax.broadcasted_iota(jnp.int32, (DIL_SUB, win), 0)
    col = lax.broadcasted_iota(jnp.int32, (DIL_SUB, win), 1)
    dist = jnp.abs(col - DIL_HALF - row)
    in_band = dist <= DIL_HALF
    dist_f = dist.astype(F32) * float(dil)
    for h in heads:
        sl = slice(h * HEAD_DIM, (h + 1) * HEAD_DIM)
        slope = 2.0 ** (-ALIBI_MAX * (group * HEADS_PER_DIL + h + 1) / N_HEADS_A)
        k = jnp.concatenate([kp_ref[:, sl], kc_ref[:, sl], kn_ref[:, sl]], axis=0)
        v = jnp.concatenate([vp_ref[:, sl], vc_ref[:, sl], vn_ref[:, sl]], axis=0)
        for j in range(DIL_TQ // DIL_SUB):
            rows = slice(j * DIL_SUB, (j + 1) * DIL_SUB)
            k_abs = i * DIL_TQ + (j * DIL_SUB - DIL_HALF) + col
            valid = in_band & (k_abs >= 0) & (k_abs < n_rows)
            s = lax.dot_general(q_ref[rows, sl], k[j * DIL_SUB:j * DIL_SUB + win],
                                (((1,), (1,)), ((), ())), preferred_element_type=F32)
            s = jnp.where(valid, s - slope * dist_f, MASK_VALUE)
            m = jnp.max(s, axis=-1, keepdims=True)
            p = jnp.exp(s - m)
            l = jnp.sum(p, axis=-1, keepdims=True)
            o = jnp.dot(p.astype(BF16), v[j * DIL_SUB:j * DIL_SUB + win], preferred_element_type=F32)
            o_ref[rows, sl] = (o / l).astype(BF16)
            lse_ref[rows, sl] = jnp.broadcast_to(m + jnp.log(l), (DIL_SUB, HEAD_DIM))


GATES_TN = 2048
DIL_REFS = 7


def _gates_dilated_kernel(u_ref, w_ref, *rest, seq_len):
    n_in = DIL_REFS * N_DIL_GROUPS
    dil_in, gates_ref, dil_out = rest[:n_in], rest[n_in], rest[n_in + 1:]
    t = pl.program_id(1) * pl.num_programs(2) + pl.program_id(2)
    rows_per_head = u_ref.shape[1] // HEADS_PER_DIL
    for h in range(HEADS_PER_DIL):
        rows = slice(h * rows_per_head, (h + 1) * rows_per_head)
        half = w_ref.shape[1] // 2
        for g, (_, dil) in enumerate(DIL_GROUPS):
            n_rows = seq_len // dil
            _dil_tile(*dil_in[DIL_REFS * g:DIL_REFS * (g + 1)], dil_out[2 * g], dil_out[2 * g + 1],
                      t % (n_rows // DIL_TQ), (h,), group=g, dil=dil, n_rows=n_rows)
            if g < 2:
                cols = slice(g * half, (g + 1) * half)
                gates_ref[0, rows, cols] = jax.nn.sigmoid(
                    jnp.dot(u_ref[0, rows], w_ref[:, cols], preferred_element_type=F32)).astype(BF16)


def _gates_and_dilated(u, w, col0, qa, ka, va):
    n_b, s, d = u.shape
    tm, tn = PROJ_TM, GATES_TN
    n_i, n_j = s // tm, 2 * d // tn
    assert col0 % tn == 0
    in_specs = [
        pl.BlockSpec((1, tm, d), lambda b, j, i: (b, i, 0)),
        pl.BlockSpec((d, tn), lambda b, j, i: (0, col0 // tn + j), pipeline_mode=pl.Buffered(1)),
    ]
    args = [u, w]
    out_specs = [pl.BlockSpec((1, tm, tn), lambda b, j, i: (b, i, j))]
    out_shape = [jax.ShapeDtypeStruct((n_b, s, 2 * d), BF16)]
    for g, (_, dil) in enumerate(DIL_GROUPS):
        n_rows = s // dil
        per_class = n_rows // DIL_TQ
        assert dil * per_class == n_j * n_i
        halo_per_blk = DIL_TQ // DIL_HALF
        n_halo = n_rows // DIL_HALF

        def tile(j, i, per_class=per_class):
            t = j * n_i + i
            return t // per_class, t % per_class

        def cur_map(b, j, i, tile=tile):
            r, k = tile(j, i)
            return (b, r, k, 0)

        def prev_map(b, j, i, tile=tile, hpb=halo_per_blk):
            r, k = tile(j, i)
            return (b, r, jnp.maximum(k * hpb - 1, 0), 0)

        def next_map(b, j, i, tile=tile, hpb=halo_per_blk, n_halo=n_halo):
            r, k = tile(j, i)
            return (b, r, jnp.minimum((k + 1) * hpb, n_halo - 1), 0)

        cur = pl.BlockSpec((None, None, DIL_TQ, D_A_OUT), cur_map)
        prev = pl.BlockSpec((None, None, DIL_HALF, D_A_OUT), prev_map)
        nxt = pl.BlockSpec((None, None, DIL_HALF, D_A_OUT), next_map)
        in_specs += [cur, prev, cur, nxt, prev, cur, nxt]
        args += [qa[g], ka[g], ka[g], ka[g], va[g], va[g], va[g]]
        out_specs += [cur, cur]
        out_shape += [jax.ShapeDtypeStruct((n_b, dil, n_rows, D_A_OUT), BF16),
                      jax.ShapeDtypeStruct((n_b, dil, n_rows, D_A_OUT), F32)]
    out = pl.pallas_call(
        functools.partial(_gates_dilated_kernel, seq_len=s),
        grid=(n_b, n_j, n_i),
        in_specs=in_specs,
        out_specs=out_specs,
        out_shape=out_shape,
        compiler_params=_params(("arbitrary", "arbitrary", "arbitrary")),
        name="gates_dilated",
    )(*args)
    return out[0], out[1::2], out[2::2]


MERGE_TM = 256


def _merge_kernel(o0_ref, o1_ref, o2_ref, l0_ref, l1_ref, l2_ref, ob_ref, sa_ref, sb_ref,
                  wa_ref, wb_ref, wo_ref, h_ref, g_ref, out_ref, nat_ref):
    tm = out_ref.shape[1]

    def natural(ref, dil, slot):
        if dil == 1:
            return ref[0].astype(F32)
        for r in range(dil):
            for h in range(HEADS_PER_DIL):
                nat_ref[slot, h, pl.ds(r, tm // dil, stride=dil), :] = (
                    ref[r, :, h * HEAD_DIM:(h + 1) * HEAD_DIM].astype(F32))
        return jnp.concatenate([nat_ref[slot, h] for h in range(HEADS_PER_DIL)], axis=1)

    gated_b = sb_ref[0].astype(F32) * jnp.dot(ob_ref[0], wb_ref[...], preferred_element_type=F32)
    dils = [dil for _, dil in DIL_GROUPS]
    l0, l1, l2 = [natural(ref, dil, n) for n, (ref, dil) in enumerate(zip((l0_ref, l1_ref, l2_ref), dils))]
    o0, o1, o2 = [natural(ref, dil, N_DIL_GROUPS + n)
                  for n, (ref, dil) in enumerate(zip((o0_ref, o1_ref, o2_ref), dils))]
    mx = jnp.maximum(jnp.maximum(l0, l1), l2)
    e0, e1, e2 = jnp.exp(l0 - mx), jnp.exp(l1 - mx), jnp.exp(l2 - mx)
    oa = (e0 * o0 + e1 * o1 + e2 * o2) / (e0 + e1 + e2)
    pa = jnp.dot(oa.astype(BF16), wa_ref[...], preferred_element_type=F32)
    merged = sa_ref[0].astype(F32) * pa + gated_b
    mixed = jnp.dot(merged.astype(BF16), wo_ref[...], preferred_element_type=F32)
    out_ref[0] = h_ref[0] + g_ref[0, 0] * mixed


def _merge(o_groups, lse_groups, out_b, gates, wa, wb, wo, h, mod4, k_gate):
    n_b, s, d = h.shape
    tm = MERGE_TM
    a_specs = [pl.BlockSpec((None, dil, tm // dil, D_A_OUT), lambda b, i: (b, 0, i, 0))
               for _, dil in DIL_GROUPS]
    row_spec = pl.BlockSpec((1, tm, d), lambda b, i: (b, i, 0))
    full = lambda shape: pl.BlockSpec(shape, lambda b, i: (0, 0))
    return pl.pallas_call(
        _merge_kernel,
        grid=(n_b, s // tm),
        in_specs=a_specs + a_specs + [
            pl.BlockSpec((1, tm, D_QB), lambda b, i: (b, i, 0)),
            pl.BlockSpec((1, tm, d), lambda b, i: (b, i, 0)),
            pl.BlockSpec((1, tm, d), lambda b, i: (b, i, 1)),
            full((D_A_OUT, d)), full((D_QB, d)), full((d, d)),
            row_spec,
            pl.BlockSpec((1, 1, 1, d), lambda b, i: (b, k_gate, 0, 0)),
        ],
        out_specs=row_spec,
        out_shape=jax.ShapeDtypeStruct((n_b, s, d), F32),
        scratch_shapes=[pltpu.VMEM((2 * N_DIL_GROUPS, HEADS_PER_DIL, tm, HEAD_DIM), F32)],
        compiler_params=_params(("parallel", "arbitrary")),
        name="mix_merge",
    )(*o_groups, *lse_groups, out_b, gates, gates, wa, wb, wo, h, mod4)


def _rope_tables(s):
    rows = s // GRID_W
    half = HEAD_DIM // 2
    inv_freq = ROPE_THETA ** (-jnp.arange(0, half, 2, dtype=F32) / half)
    ang_row = jnp.arange(rows).astype(F32)[:, None] * inv_freq
    ang_col = jnp.arange(GRID_W).astype(F32)[:, None] * inv_freq

    def expand(fn):
        return jnp.concatenate([jnp.repeat(fn(ang_row), GRID_W, axis=0),
                                jnp.tile(fn(ang_col), (rows, 1))], axis=-1)

    cos, sin = expand(jnp.cos), expand(jnp.sin)
    return jnp.concatenate([cos, cos], axis=-1), jnp.concatenate([-sin, sin], axis=-1)


def kernel(x, c, w_ada, b_ada, norm_ffn1, w1_ffn1, w3_ffn1, w2_ffn1, norm_mix, w_in, q_norm_a, k_norm_a, q_norm_b, k_norm_b, w_branch_a, w_branch_b, w_out, norm_ffn2, w1_ffn2, w3_ffn2, w2_ffn2, norm_final):
    n_b, s, d = x.shape
    assert w_ada.shape[0] == 1, "the fused pipeline covers a single layer"
    cos_f, sin_f = _rope_tables(s)
    h = x
    for l in range(1):
        mod4 = _ada(c, w_ada[l], b_ada[l]).reshape(n_b, N_MOD, 1, d)
        h, u, wi, wa, wb, wo = _ffn(
            h, mod4, norm_ffn1[l], w1_ffn1[l].astype(BF16), w3_ffn1[l].astype(BF16),
            w2_ffn1[l].astype(BF16), (0, 1, 2), "mid", norm_mix[l], (3, 4),
            to_cast=[w_in[l], w_branch_a[l], w_branch_b[l], w_out[l]])
        o_qb = 3 * D_A
        o_kb = o_qb + D_QB
        o_vb = o_kb + D_KVB
        o_gates = o_vb + D_KVB
        gain_qb = jnp.tile(q_norm_b[l], N_Q_B)[None]
        gain_kb = jnp.tile(k_norm_b[l], N_KV_B)[None]
        qa = _proj_classes(u, wi, 0, q_norm_a[l], scale=SM_SCALE)
        ka = _proj_classes(u, wi, 1, k_norm_a[l])
        va = _proj_classes(u, wi, 2)
        qb = _proj(u, wi, o_qb, D_QB, "norm_rope", 512, gain_qb, cos_f, sin_f, scale=SM_SCALE * LOG2E)
        kb = _proj(u, wi, o_kb, D_KVB, "norm_rope", 256, gain_kb, cos_f, sin_f)
        vb = _proj(u, wi, o_vb, D_KVB, "chunk_t", 256)
        gates, o_groups, lse_groups = _gates_and_dilated(u, wi, o_gates, qa, ka, va)
        out_b, ffn2_w = _gqa(qb, kb, vb, [w1_ffn2[l], w3_ffn2[l], w2_ffn2[l]])
        h = _merge(o_groups, lse_groups, out_b, gates, wa, wb, wo, h, mod4, 5)
        (h,) = _ffn(h, mod4, norm_ffn2[l], *ffn2_w, (6, 7, 8), "final", norm_final)
    return h
```

```python
import functools

import jax
import jax.numpy as jnp
from jax import lax
from jax.experimental import pallas as pl
from jax.experimental.pallas import tpu as pltpu

F32 = jnp.float32
BF16 = jnp.bfloat16

D_MODEL = 2048
D_FF = 5632
HEAD_DIM = 128
DIL_GROUPS = ((128, 1), (512, 4), (2048, 16))
N_DIL_GROUPS = 3
HEADS_PER_DIL = 4
N_HEADS_A = N_DIL_GROUPS * HEADS_PER_DIL
N_Q_B = 8
N_KV_B = 2
Q_PER_KV = N_Q_B // N_KV_B
GRID_W = 64
ROPE_THETA = 10000.0
ALIBI_MAX = 8.0
N_MOD = 9
EPS = 1e-6
D_A = N_HEADS_A * HEAD_DIM
D_A_OUT = HEADS_PER_DIL * HEAD_DIM
D_QB = N_Q_B * HEAD_DIM
D_KVB = N_KV_B * HEAD_DIM
SM_SCALE = HEAD_DIM ** -0.5
MASK_VALUE = -1e30

LANES = 128
BF16_SUBLANES = 16
VMEM_LIMIT = 56 * 1024 * 1024


def _params(sem):
    return pltpu.CompilerParams(dimension_semantics=sem, vmem_limit_bytes=VMEM_LIMIT)


def _silu(x):
    return x * jax.nn.sigmoid(x)


def _rms(x, gain):
    ms = jnp.mean(x * x, axis=-1, keepdims=True)
    return x * lax.rsqrt(ms + EPS) * gain


def _rms_head(x, gain):
    sq = x * x
    hi = sq.astype(BF16)
    lo = (sq - hi.astype(F32)).astype(BF16)
    ones = jnp.ones((HEAD_DIM, HEAD_DIM), BF16)
    ssq = (jnp.dot(hi, ones, preferred_element_type=F32) + jnp.dot(lo, ones, preferred_element_type=F32))
    return x * lax.rsqrt(ssq * (1.0 / HEAD_DIM) + EPS) * gain


ADA_TN = 2048


def _ada_kernel(c_ref, w_ref, b_ref, o_ref):
    n_b = c_ref.shape[0]
    for b in range(n_b):
        cs = _silu(c_ref[b])
        for j in range(ADA_TN // LANES):
            sl = slice(j * LANES, (j + 1) * LANES)
            acc = jnp.sum(w_ref[:, sl] * cs, axis=0, keepdims=True)
            o_ref[b, :, sl] = acc + b_ref[:, sl]


def _ada(c, w, b):
    n_b, d = c.shape
    n = w.shape[1]
    c_b = jnp.broadcast_to(c[:, :, None], (n_b, d, LANES))
    return pl.pallas_call(
        _ada_kernel,
        grid=(n // ADA_TN,),
        in_specs=[
            pl.BlockSpec((n_b, d, LANES), lambda j: (0, 0, 0)),
            pl.BlockSpec((d, ADA_TN), lambda j: (0, j)),
            pl.BlockSpec((1, ADA_TN), lambda j: (0, j)),
        ],
        out_specs=pl.BlockSpec((n_b, 1, ADA_TN), lambda j: (0, 0, j)),
        out_shape=jax.ShapeDtypeStruct((n_b, 1, n), F32),
        compiler_params=_params(("arbitrary",)),
        name="ada_mod",
    )(c_b, w, b.reshape(1, n))


FFN_TM = 512
FFN_TF = 512
FFN_FIRST_ROWS = (256, 256)
FFN_LAST_ROWS = (256, 256)


def _ffn_kernel(x_ref, gain_ref, sh_ref, sc_ref, g_ref, w1_ref, w3_ref, w2_ref, *rest, mode, n_cast):
    if mode == "mid":
        gain2_ref, sh2_ref, sc2_ref = rest[:3]
        f32_refs, (h_ref, u2_ref) = rest[3:3 + n_cast], rest[3 + n_cast:5 + n_cast]
        bf16_refs, (u_scr, acc_scr) = rest[5 + n_cast:5 + 2 * n_cast], rest[5 + 2 * n_cast:]
    else:
        gainf_ref = rest[0]
        f32_refs, y_ref = rest[1:1 + n_cast], rest[1 + n_cast]
        bf16_refs, (u_scr, acc_scr) = rest[2 + n_cast:2 + 2 * n_cast], rest[2 + 2 * n_cast:]
    for src, dst in zip(f32_refs, bf16_refs):
        dst[...] = src[...].astype(BF16)
    f = pl.program_id(2)
    nf = pl.num_programs(2)
    def chunks(sizes):
        starts = [sum(sizes[:k]) for k in range(len(sizes))]
        return [slice(r, r + n) for r, n in zip(starts, sizes)]

    def swiglu_part(u):
        a = jnp.dot(u, w1_ref[...], preferred_element_type=F32)
        b = jnp.dot(u, w3_ref[...], preferred_element_type=F32)
        hm = (_silu(a) * b).astype(BF16)
        return jnp.dot(hm, w2_ref[...], preferred_element_type=F32)

    @pl.when(f == 0)
    def _():
        gain_mod = gain_ref[...] * (1.0 + sc_ref[0, 0])
        for rows in chunks(FFN_FIRST_ROWS):
            u = (_rms(x_ref[0, rows], gain_mod) + sh_ref[0, 0]).astype(BF16)
            u_scr[rows] = u
            acc_scr[rows] = swiglu_part(u)

    @pl.when((f > 0) & (f < nf - 1))
    def _():
        acc_scr[...] += swiglu_part(u_scr[...])

    @pl.when(f == nf - 1)
    def _():
        if mode == "mid":
            gain2_mod = gain2_ref[...] * (1.0 + sc2_ref[0, 0])
        for rows in chunks(FFN_LAST_ROWS):
            total = acc_scr[rows] + swiglu_part(u_scr[rows])
            h = x_ref[0, rows] + (0.5 * g_ref[0, 0]) * total
            if mode == "mid":
                h_ref[0, rows] = h
                u2_ref[0, rows] = (_rms(h, gain2_mod) + sh2_ref[0, 0]).astype(BF16)
            else:
                y_ref[0, rows] = _rms(h, gainf_ref[...])


def _mod_spec(k):
    return pl.BlockSpec((1, 1, 1, D_MODEL), lambda b, i, f, k=k: (b, k, 0, 0))


def _ffn_cast_spec(shape, n_b, n_i, n_f):
    rows = shape[0] // (n_b * n_i)
    assert rows * n_b * n_i == shape[0] and rows % BF16_SUBLANES == 0
    lane_blocks = shape[1] // LANES
    n_col = max(k for k in range(1, n_f + 1) if lane_blocks % k == 0)
    return pl.BlockSpec((rows, shape[1] // n_col),
                        lambda b, i, f: (b * n_i + i, jnp.minimum(f, n_col - 1)))


def _ffn(x, mod4, gain, w1, w3, w2, ks, mode, gain_next, ks_next=None, to_cast=()):
    n_b, s, d = x.shape
    tm, tf = FFN_TM, FFN_TF
    grid = (n_b, s // tm, D_FF // tf)
    cast_specs = [_ffn_cast_spec(arr.shape, *grid) for arr in to_cast]
    cast_shapes = [jax.ShapeDtypeStruct(arr.shape, BF16) for arr in to_cast]
    row_spec = pl.BlockSpec((1, tm, d), lambda b, i, f: (b, i, 0))
    vec_spec = pl.BlockSpec((1, d), lambda b, i, f: (0, 0))
    in_specs = [
        row_spec, vec_spec, _mod_spec(ks[0]), _mod_spec(ks[1]), _mod_spec(ks[2]),
        pl.BlockSpec((d, tf), lambda b, i, f: (0, f)),
        pl.BlockSpec((d, tf), lambda b, i, f: (0, f)),
        pl.BlockSpec((tf, d), lambda b, i, f: (f, 0)),
        vec_spec,
    ]
    args = [x, gain.reshape(1, d), mod4, mod4, mod4, w1, w3, w2, gain_next.reshape(1, d)]
    if mode == "mid":
        in_specs += [_mod_spec(ks_next[0]), _mod_spec(ks_next[1])]
        args += [mod4, mod4]
        out_specs = [row_spec, row_spec]
        out_shape = [jax.ShapeDtypeStruct((n_b, s, d), F32), jax.ShapeDtypeStruct((n_b, s, d), BF16)]
    else:
        out_specs = [row_spec]
        out_shape = [jax.ShapeDtypeStruct((n_b, s, d), F32)]
    return pl.pallas_call(
        functools.partial(_ffn_kernel, mode=mode, n_cast=len(to_cast)),
        grid=grid,
        in_specs=in_specs + cast_specs,
        out_specs=out_specs + cast_specs,
        out_shape=out_shape + cast_shapes,
        scratch_shapes=[pltpu.VMEM((tm, d), BF16), pltpu.VMEM((tm, d), F32)],
        compiler_params=_params(("arbitrary", "arbitrary", "arbitrary")),
        name="ffn_" + mode,
    )(*args, *to_cast)


PROJ_TM = 1024


def _proj_b_kernel(u_ref, w_ref, gain_ref, cos_ref, sin_ref, q_ref, k_ref, vt_ref, *, q_scale):
    acc = jnp.dot(u_ref[0], w_ref[...], preferred_element_type=F32)
    tm = acc.shape[0]
    for j in range(N_Q_B + N_KV_B):
        sl = slice(j * HEAD_DIM, (j + 1) * HEAD_DIM)
        y = _rms_head(acc[:, sl], gain_ref[:, sl])
        y = y * cos_ref[...] + pltpu.roll(y, HEAD_DIM // 2, axis=1) * sin_ref[...]
        if j < N_Q_B:
            q_ref[0, :, sl] = (y * q_scale).astype(BF16)
        else:
            k_ref[0, :, (j - N_Q_B) * HEAD_DIM:(j - N_Q_B + 1) * HEAD_DIM] = y.astype(BF16)
    v_t = acc[:, D_QB + D_KVB:].T.astype(BF16)
    for hh in range(N_KV_B):
        for cc in range(tm // GQA_TK):
            vt_ref[0, hh, cc] = v_t[hh * HEAD_DIM:(hh + 1) * HEAD_DIM, cc * GQA_TK:(cc + 1) * GQA_TK]


def _proj_b(u, w, col0, gain_row, cos, sin, q_scale):
    n_b, s, d = u.shape
    tm = PROJ_TM
    n = D_QB + 2 * D_KVB
    assert col0 % n == 0
    return pl.pallas_call(
        functools.partial(_proj_b_kernel, q_scale=q_scale),
        grid=(n_b, s // tm),
        in_specs=[
            pl.BlockSpec((1, tm, d), lambda b, i: (b, i, 0)),
            pl.BlockSpec((d, n), lambda b, i: (0, col0 // n)),
            pl.BlockSpec((1, D_QB + D_KVB), lambda b, i: (0, 0)),
            pl.BlockSpec((tm, HEAD_DIM), lambda b, i: (i, 0)),
            pl.BlockSpec((tm, HEAD_DIM), lambda b, i: (i, 0)),
        ],
        out_specs=[
            pl.BlockSpec((1, tm, D_QB), lambda b, i: (b, i, 0)),
            pl.BlockSpec((1, tm, D_KVB), lambda b, i: (b, i, 0)),
            pl.BlockSpec((1, N_KV_B, tm // GQA_TK, HEAD_DIM, GQA_TK), lambda b, i: (b, 0, i, 0, 0)),
        ],
        out_shape=[
            jax.ShapeDtypeStruct((n_b, s, D_QB), BF16),
            jax.ShapeDtypeStruct((n_b, s, D_KVB), BF16),
            jax.ShapeDtypeStruct((n_b, N_KV_B, s // GQA_TK, HEAD_DIM, GQA_TK), BF16),
        ],
        compiler_params=_params(("parallel", "arbitrary")),
        name="proj_mixer_b",
    )(u, w, gain_row, cos, sin)


def _proj_classes_kernel(u_ref, w_ref, *rest, norm, scale):
    if norm:
        gain_ref, rest = rest[0], rest[1:]
    o_refs, scr = rest[:N_DIL_GROUPS], rest[N_DIL_GROUPS]
    acc = jnp.dot(u_ref[0], w_ref[...], preferred_element_type=F32)
    tm = acc.shape[0]
    for g, (_, dil) in enumerate(DIL_GROUPS):
        for h in range(HEADS_PER_DIL):
            head = g * HEADS_PER_DIL + h
            lanes = slice(h * HEAD_DIM, (h + 1) * HEAD_DIM)
            y = acc[:, head * HEAD_DIM:(head + 1) * HEAD_DIM]
            if norm:
                y = _rms_head(y, gain_ref[...])
            if scale != 1.0:
                y = y * scale
            if dil == 1:
                o_refs[g][0, 0, :, lanes] = y.astype(BF16)
                continue
            scr[head] = y
            for r in range(dil):
                o_refs[g][0, r, :, lanes] = scr[head, pl.ds(r, tm // dil, stride=dil), :].astype(BF16)


def _proj_classes(u, w, which, gain=None, scale=1.0):
    n_b, s, d = u.shape
    tm = PROJ_TM
    norm = gain is not None
    in_specs = [
        pl.BlockSpec((1, tm, d), lambda b, i: (b, i, 0)),
        pl.BlockSpec((d, D_A), lambda b, i: (0, which)),
    ]
    args = [u, w]
    if norm:
        in_specs.append(pl.BlockSpec((1, HEAD_DIM), lambda b, i: (0, 0)))
        args.append(gain.reshape(1, HEAD_DIM))
    return pl.pallas_call(
        functools.partial(_proj_classes_kernel, norm=norm, scale=scale),
        grid=(n_b, s // tm),
        in_specs=in_specs,
        out_specs=[pl.BlockSpec((1, dil, tm // dil, D_A_OUT), lambda b, i: (b, 0, i, 0))
                   for _, dil in DIL_GROUPS],
        out_shape=[jax.ShapeDtypeStruct((n_b, dil, s // dil, D_A_OUT), BF16) for _, dil in DIL_GROUPS],
        scratch_shapes=[pltpu.VMEM((N_HEADS_A, tm, HEAD_DIM), F32)],
        compiler_params=_params(("parallel", "arbitrary")),
        name="proj_classes",
    )(*args)


GQA_TQ = 512
GQA_TK = 512
LOG2E = 1.4426950408889634


def _gqa_kernel(q_ref, qn_ref, k_ref, vt_ref, *rest, n_cast):
    f32_refs, o_ref = rest[:n_cast], rest[n_cast]
    bf16_refs = rest[n_cast + 1:2 * n_cast + 1]
    acc_ref, s_ref = rest[2 * n_cast + 1:]
    for src, dst in zip(f32_refs, bf16_refs):
        dst[...] = src[...].astype(BF16)
    n_chunks = vt_ref.shape[2]
    tq = q_ref.shape[1]
    heads = range(Q_PER_KV)
    nt_dims = (((1,), (1,)), ((), ()))

    def k_chunk(c):
        return k_ref[0, pl.ds(pl.multiple_of(c * GQA_TK, GQA_TK), GQA_TK), :]

    def q_head(g, ref=q_ref):
        return ref[0, :, g * HEAD_DIM:(g + 1) * HEAD_DIM]

    ones_rows = jnp.ones((BF16_SUBLANES, GQA_TK), BF16)

    def step(c, c_next, slot, ms, next_q_ref=q_ref):
        k_next = k_chunk(c_next)
        vt = jnp.concatenate([vt_ref[0, 0, c], ones_rows], axis=0)
        new_ms = []
        for g in heads:
            s_ref[1 - slot, g] = lax.dot_general(k_next, q_head(g, next_q_ref), nt_dims,
                                                 preferred_element_type=F32)
            m_new = jnp.maximum(ms[g], jnp.max(s_ref[slot, g], axis=0, keepdims=True))
            alpha = jnp.exp2(ms[g] - m_new)
            p = jnp.exp2(s_ref[slot, g] - m_new).astype(BF16)
            new_ms.append(m_new)
            acc_ref[g] = alpha * acc_ref[g] + jnp.dot(vt, p, preferred_element_type=F32)
        return tuple(new_ms)

    acc_ref[...] = jnp.zeros(acc_ref.shape, F32)

    @pl.when(pl.program_id(2) == 0)
    def _():
        k0 = k_chunk(0)
        for g in heads:
            s_ref[0, g] = lax.dot_general(k0, q_head(g), nt_dims, preferred_element_type=F32)

    def body(i, ms):
        ms = step(2 * i, 2 * i + 1, 0, ms)
        return step(2 * i + 1, 2 * i + 2, 1, ms)

    init = tuple(jnp.full((1, tq), MASK_VALUE, F32) for _ in heads)
    ms = lax.fori_loop(0, n_chunks // 2 - 1, body, init)
    ms = step(n_chunks - 2, n_chunks - 1, 0, ms)
    step(n_chunks - 1, 0, 1, ms, qn_ref)
    for g in heads:
        o_t = acc_ref[g, :HEAD_DIM] / acc_ref[g, HEAD_DIM:HEAD_DIM + 1]
        o_ref[0, :, g * HEAD_DIM:(g + 1) * HEAD_DIM] = o_t.T.astype(BF16)


def _cast_block(n_rows, n_steps):
    per = 1
    while n_rows % (n_steps // per) or (n_rows // (n_steps // per)) % BF16_SUBLANES:
        per *= 2
        assert per <= n_steps
    return n_rows // (n_steps // per), per


def _gqa(qb, kb, vbt, to_cast):
    n_b, s, _ = qb.shape
    tq = GQA_TQ
    w = Q_PER_KV * HEAD_DIM
    nq = s // tq
    n_steps = n_b * N_KV_B * nq
    cast_specs = []
    for arr in to_cast:
        rows, per = _cast_block(arr.shape[0], n_steps)
        cast_specs.append(pl.BlockSpec(
            (rows, arr.shape[1]), lambda b, h, i, per=per: (((b * N_KV_B + h) * nq + i) // per, 0)))
    out = pl.pallas_call(
        functools.partial(_gqa_kernel, n_cast=len(to_cast)),
        grid=(n_b, N_KV_B, nq),
        in_specs=[
            pl.BlockSpec((1, tq, w), lambda b, h, i: (b, i, h)),
            pl.BlockSpec((1, tq, w), lambda b, h, i: (b, jnp.minimum(i + 1, nq - 1), h)),
            pl.BlockSpec((1, s, HEAD_DIM), lambda b, h, i: (b, 0, h)),
            pl.BlockSpec((1, 1, s // GQA_TK, HEAD_DIM, GQA_TK), lambda b, h, i: (b, h, 0, 0, 0)),
        ] + cast_specs,
        out_specs=[pl.BlockSpec((1, tq, w), lambda b, h, i: (b, i, h))] + cast_specs,
        out_shape=[jax.ShapeDtypeStruct((n_b, s, D_QB), BF16)]
        + [jax.ShapeDtypeStruct(arr.shape, BF16) for arr in to_cast],
        scratch_shapes=[pltpu.VMEM((Q_PER_KV, HEAD_DIM + BF16_SUBLANES, tq), F32),
                        pltpu.VMEM((2, Q_PER_KV, GQA_TK, tq), F32)],
        compiler_params=_params(("arbitrary", "arbitrary", "arbitrary")),
        name="gqa_attn",
    )(qb, qb, kb, vbt, *to_cast)
    return out[0], out[1:]


DIL_TQ = 512
DIL_SUB = 128
DIL_HALF = 64


def _dil_tile(q_ref, kp_ref, kc_ref, kn_ref, vp_ref, vc_ref, vn_ref, o_ref, lse_ref, i, heads, *,
              group, dil, n_rows):
    win = DIL_SUB + 2 * DIL_HALF
    row = lax.broadcasted_iota(jnp.int32, (DIL_SUB, win), 0)
    col = lax.broadcasted_iota(jnp.int32, (DIL_SUB, win), 1)
    dist = jnp.abs(col - DIL_HALF - row)
    in_band = dist <= DIL_HALF
    dist_f = dist.astype(F32) * float(dil)
    for h in heads:
        sl = slice(h * HEAD_DIM, (h + 1) * HEAD_DIM)
        slope = 2.0 ** (-ALIBI_MAX * (group * HEADS_PER_DIL + h + 1) / N_HEADS_A)
        k = jnp.concatenate([kp_ref[:, sl], kc_ref[:, sl], kn_ref[:, sl]], axis=0)
        v = jnp.concatenate([vp_ref[:, sl], vc_ref[:, sl], vn_ref[:, sl]], axis=0)
        for j in range(DIL_TQ // DIL_SUB):
            rows = slice(j * DIL_SUB, (j + 1) * DIL_SUB)
            k_abs = i * DIL_TQ + (j * DIL_SUB - DIL_HALF) + col
            valid = in_band & (k_abs >= 0) & (k_abs < n_rows)
            s = lax.dot_general(q_ref[rows, sl], k[j * DIL_SUB:j * DIL_SUB + win],
                                (((1,), (1,)), ((), ())), preferred_element_type=F32)
            s = jnp.where(valid, s - slope * dist_f, MASK_VALUE)
            m = jnp.max(s, axis=-1, keepdims=True)
            p = jnp.exp(s - m)
            l = jnp.sum(p, axis=-1, keepdims=True)
            o = jnp.dot(p.astype(BF16), v[j * DIL_SUB:j * DIL_SUB + win], preferred_element_type=F32)
            o_ref[rows, sl] = (o / l).astype(BF16)
            lse_ref[rows, sl] = jnp.broadcast_to(m + jnp.log(l), (DIL_SUB, HEAD_DIM))


GATES_TN = 2048
DIL_REFS = 7


def _gates_dilated_kernel(u_ref, w_ref, *rest, seq_len):
    n_in = DIL_REFS * N_DIL_GROUPS
    dil_in, gates_ref, dil_out = rest[:n_in], rest[n_in], rest[n_in + 1:]
    t = pl.program_id(1) * pl.num_programs(2) + pl.program_id(2)
    rows_per_head = u_ref.shape[1] // HEADS_PER_DIL
    for h in range(HEADS_PER_DIL):
        rows = slice(h * rows_per_head, (h + 1) * rows_per_head)
        half = w_ref.shape[1] // 2
        for g, (_, dil) in enumerate(DIL_GROUPS):
            n_rows = seq_len // dil
            _dil_tile(*dil_in[DIL_REFS * g:DIL_REFS * (g + 1)], dil_out[2 * g], dil_out[2 * g + 1],
                      t % (n_rows // DIL_TQ), (h,), group=g, dil=dil, n_rows=n_rows)
            if g < 2:
                cols = slice(g * half, (g + 1) * half)
                gates_ref[0, rows, cols] = jax.nn.sigmoid(
                    jnp.dot(u_ref[0, rows], w_ref[:, cols], preferred_element_type=F32)).astype(BF16)


def _gates_and_dilated(u, w, col0, qa, ka, va):
    n_b, s, d = u.shape
    tm, tn = PROJ_TM, GATES_TN
    n_i, n_j = s // tm, 2 * d // tn
    assert col0 % tn == 0
    in_specs = [
        pl.BlockSpec((1, tm, d), lambda b, j, i: (b, i, 0)),
        pl.BlockSpec((d, tn), lambda b, j, i: (0, col0 // tn + j), pipeline_mode=pl.Buffered(1)),
    ]
    args = [u, w]
    out_specs = [pl.BlockSpec((1, tm, tn), lambda b, j, i: (b, i, j))]
    out_shape = [jax.ShapeDtypeStruct((n_b, s, 2 * d), BF16)]
    for g, (_, dil) in enumerate(DIL_GROUPS):
        n_rows = s // dil
        per_class = n_rows // DIL_TQ
        assert dil * per_class == n_j * n_i
        halo_per_blk = DIL_TQ // DIL_HALF
        n_halo = n_rows // DIL_HALF

        def tile(j, i, per_class=per_class):
            t = j * n_i + i
            return t // per_class, t % per_class

        def cur_map(b, j, i, tile=tile):
            r, k = tile(j, i)
            return (b, r, k, 0)

        def prev_map(b, j, i, tile=tile, hpb=halo_per_blk):
            r, k = tile(j, i)
            return (b, r, jnp.maximum(k * hpb - 1, 0), 0)

        def next_map(b, j, i, tile=tile, hpb=halo_per_blk, n_halo=n_halo):
            r, k = tile(j, i)
            return (b, r, jnp.minimum((k + 1) * hpb, n_halo - 1), 0)

        cur = pl.BlockSpec((None, None, DIL_TQ, D_A_OUT), cur_map)
        prev = pl.BlockSpec((None, None, DIL_HALF, D_A_OUT), prev_map)
        nxt = pl.BlockSpec((None, None, DIL_HALF, D_A_OUT), next_map)
        in_specs += [cur, prev, cur, nxt, prev, cur, nxt]
        args += [qa[g], ka[g], ka[g], ka[g], va[g], va[g], va[g]]
        out_specs += [cur, cur]
        out_shape += [jax.ShapeDtypeStruct((n_b, dil, n_rows, D_A_OUT), BF16),
                      jax.ShapeDtypeStruct((n_b, dil, n_rows, D_A_OUT), F32)]
    out = pl.pallas_call(
        functools.partial(_gates_dilated_kernel, seq_len=s),
        grid=(n_b, n_j, n_i),
        in_specs=in_specs,
        out_specs=out_specs,
        out_shape=out_shape,
        compiler_params=_params(("arbitrary", "arbitrary", "arbitrary")),
        name="gates_dilated",
    )(*args)
    return out[0], out[1::2], out[2::2]


MERGE_TM = 256


def _merge_kernel(o0_ref, o1_ref, o2_ref, l0_ref, l1_ref, l2_ref, ob_ref, sa_ref, sb_ref,
                  wa_ref, wb_ref, wo_ref, h_ref, g_ref, out_ref, nat_ref):
    tm = out_ref.shape[1]

    def natural(ref, dil, slot):
        if dil == 1:
            return ref[0].astype(F32)
        for r in range(dil):
            for h in range(HEADS_PER_DIL):
                nat_ref[slot, h, pl.ds(r, tm // dil, stride=dil), :] = (
                    ref[r, :, h * HEAD_DIM:(h + 1) * HEAD_DIM].astype(F32))
        return jnp.concatenate([nat_ref[slot, h] for h in range(HEADS_PER_DIL)], axis=1)

    gated_b = sb_ref[0].astype(F32) * jnp.dot(ob_ref[0], wb_ref[...], preferred_element_type=F32)
    dils = [dil for _, dil in DIL_GROUPS]
    l0, l1, l2 = [natural(ref, dil, n) for n, (ref, dil) in enumerate(zip((l0_ref, l1_ref, l2_ref), dils))]
    o0, o1, o2 = [natural(ref, dil, N_DIL_GROUPS + n)
                  for n, (ref, dil) in enumerate(zip((o0_ref, o1_ref, o2_ref), dils))]
    mx = jnp.maximum(jnp.maximum(l0, l1), l2)
    e0, e1, e2 = jnp.exp(l0 - mx), jnp.exp(l1 - mx), jnp.exp(l2 - mx)
    oa = (e0 * o0 + e1 * o1 + e2 * o2) / (e0 + e1 + e2)
    pa = jnp.dot(oa.astype(BF16), wa_ref[...], preferred_element_type=F32)
    merged = sa_ref[0].astype(F32) * pa + gated_b
    mixed = jnp.dot(merged.astype(BF16), wo_ref[...], preferred_element_type=F32)
    out_ref[0] = h_ref[0] + g_ref[0, 0] * mixed


def _merge(o_groups, lse_groups, out_b, gates, wa, wb, wo, h, mod4, k_gate):
    n_b, s, d = h.shape
    tm = MERGE_TM
    a_specs = [pl.BlockSpec((None, dil, tm // dil, D_A_OUT), lambda b, i: (b, 0, i, 0))
               for _, dil in DIL_GROUPS]
    row_spec = pl.BlockSpec((1, tm, d), lambda b, i: (b, i, 0))
    full = lambda shape: pl.BlockSpec(shape, lambda b, i: (0, 0))
    return pl.pallas_call(
        _merge_kernel,
        grid=(n_b, s // tm),
        in_specs=a_specs + a_specs + [
            pl.BlockSpec((1, tm, D_QB), lambda b, i: (b, i, 0)),
            pl.BlockSpec((1, tm, d), lambda b, i: (b, i, 0)),
            pl.BlockSpec((1, tm, d), lambda b, i: (b, i, 1)),
            full((D_A_OUT, d)), full((D_QB, d)), full((d, d)),
            row_spec,
            pl.BlockSpec((1, 1, 1, d), lambda b, i: (b, k_gate, 0, 0)),
        ],
        out_specs=row_spec,
        out_shape=jax.ShapeDtypeStruct((n_b, s, d), F32),
        scratch_shapes=[pltpu.VMEM((2 * N_DIL_GROUPS, HEADS_PER_DIL, tm, HEAD_DIM), F32)],
        compiler_params=_params(("parallel", "arbitrary")),
        name="mix_merge",
    )(*o_groups, *lse_groups, out_b, gates, gates, wa, wb, wo, h, mod4)


def _rope_tables(s):
    rows = s // GRID_W
    half = HEAD_DIM // 2
    inv_freq = ROPE_THETA ** (-jnp.arange(0, half, 2, dtype=F32) / half)
    ang_row = jnp.arange(rows).astype(F32)[:, None] * inv_freq
    ang_col = jnp.arange(GRID_W).astype(F32)[:, None] * inv_freq

    def expand(fn):
        return jnp.concatenate([jnp.repeat(fn(ang_row), GRID_W, axis=0),
                                jnp.tile(fn(ang_col), (rows, 1))], axis=-1)

    cos, sin = expand(jnp.cos), expand(jnp.sin)
    return jnp.concatenate([cos, cos], axis=-1), jnp.concatenate([-sin, sin], axis=-1)


def kernel(x, c, w_ada, b_ada, norm_ffn1, w1_ffn1, w3_ffn1, w2_ffn1, norm_mix, w_in, q_norm_a, k_norm_a, q_norm_b, k_norm_b, w_branch_a, w_branch_b, w_out, norm_ffn2, w1_ffn2, w3_ffn2, w2_ffn2, norm_final):
    n_b, s, d = x.shape
    assert w_ada.shape[0] == 1, "the fused pipeline covers a single layer"
    cos_f, sin_f = _rope_tables(s)
    h = x
    for l in range(1):
        mod4 = _ada(c, w_ada[l], b_ada[l]).reshape(n_b, N_MOD, 1, d)
        h, u, wi, wa, wb, wo = _ffn(
            h, mod4, norm_ffn1[l], w1_ffn1[l].astype(BF16), w3_ffn1[l].astype(BF16),
            w2_ffn1[l].astype(BF16), (0, 1, 2), "mid", norm_mix[l], (3, 4),
            to_cast=[w_in[l], w_branch_a[l], w_branch_b[l], w_out[l]])
        o_qb = 3 * D_A
        o_gates = o_qb + D_QB + 2 * D_KVB
        gain_b = jnp.concatenate([jnp.tile(q_norm_b[l], N_Q_B), jnp.tile(k_norm_b[l], N_KV_B)])[None]
        qa = _proj_classes(u, wi, 0, q_norm_a[l], scale=SM_SCALE)
        ka = _proj_classes(u, wi, 1, k_norm_a[l])
        va = _proj_classes(u, wi, 2)
        qb, kb, vb = _proj_b(u, wi, o_qb, gain_b, cos_f, sin_f, SM_SCALE * LOG2E)
        gates, o_groups, lse_groups = _gates_and_dilated(u, wi, o_gates, qa, ka, va)
        out_b, ffn2_w = _gqa(qb, kb, vb, [w1_ffn2[l], w3_ffn2[l], w2_ffn2[l]])
        h = _merge(o_groups, lse_groups, out_b, gates, wa, wb, wo, h, mod4, 5)
        (h,) = _ffn(h, mod4, norm_ffn2[l], *ffn2_w, (6, 7, 8), "final", norm_final)
    return h
```

```python
import functools

import jax
import jax.numpy as jnp
from jax import lax
from jax.experimental import pallas as pl
from jax.experimental.pallas import tpu as pltpu

F32 = jnp.float32
BF16 = jnp.bfloat16

D_MODEL = 2048
D_FF = 5632
HEAD_DIM = 128
DIL_GROUPS = ((128, 1), (512, 4), (2048, 16))
N_DIL_GROUPS = 3
HEADS_PER_DIL = 4
N_HEADS_A = N_DIL_GROUPS * HEADS_PER_DIL
N_Q_B = 8
N_KV_B = 2
Q_PER_KV = N_Q_B // N_KV_B
GRID_W = 64
ROPE_THETA = 10000.0
ALIBI_MAX = 8.0
N_MOD = 9
EPS = 1e-6
D_A = N_HEADS_A * HEAD_DIM
D_A_OUT = HEADS_PER_DIL * HEAD_DIM
D_QB = N_Q_B * HEAD_DIM
D_KVB = N_KV_B * HEAD_DIM
SM_SCALE = HEAD_DIM ** -0.5
MASK_VALUE = -1e30

LANES = 128
BF16_SUBLANES = 16
VMEM_LIMIT = 56 * 1024 * 1024


def _params(sem):
    return pltpu.CompilerParams(dimension_semantics=sem, vmem_limit_bytes=VMEM_LIMIT)


def _silu(x):
    return x * jax.nn.sigmoid(x)


def _rms(x, gain):
    ms = jnp.mean(x * x, axis=-1, keepdims=True)
    return x * lax.rsqrt(ms + EPS) * gain


def _rms_head(x, gain):
    sq = x * x
    hi = sq.astype(BF16)
    lo = (sq - hi.astype(F32)).astype(BF16)
    ones = jnp.ones((HEAD_DIM, HEAD_DIM), BF16)
    ssq = (jnp.dot(hi, ones, preferred_element_type=F32) + jnp.dot(lo, ones, preferred_element_type=F32))
    return x * lax.rsqrt(ssq * (1.0 / HEAD_DIM) + EPS) * gain


ADA_TN = 2048


def _ada_kernel(c_ref, w_ref, b_ref, o_ref):
    n_b = c_ref.shape[0]
    for b in range(n_b):
        cs = _silu(c_ref[b])
        for j in range(ADA_TN // LANES):
            sl = slice(j * LANES, (j + 1) * LANES)
            acc = jnp.sum(w_ref[:, sl] * cs, axis=0, keepdims=True)
            o_ref[b, :, sl] = acc + b_ref[:, sl]


def _ada(c, w, b):
    n_b, d = c.shape
    n = w.shape[1]
    c_b = jnp.broadcast_to(c[:, :, None], (n_b, d, LANES))
    return pl.pallas_call(
        _ada_kernel,
        grid=(n // ADA_TN,),
        in_specs=[
            pl.BlockSpec((n_b, d, LANES), lambda j: (0, 0, 0)),
            pl.BlockSpec((d, ADA_TN), lambda j: (0, j)),
            pl.BlockSpec((1, ADA_TN), lambda j: (0, j)),
        ],
        out_specs=pl.BlockSpec((n_b, 1, ADA_TN), lambda j: (0, 0, j)),
        out_shape=jax.ShapeDtypeStruct((n_b, 1, n), F32),
        compiler_params=_params(("arbitrary",)),
        name="ada_mod",
    )(c_b, w, b.reshape(1, n))


FFN_TM = 512
FFN_TF = 512
FFN_FIRST_ROWS = (256, 256)
FFN_LAST_ROWS = (256, 256)


def _ffn_kernel(x_ref, gain_ref, sh_ref, sc_ref, g_ref, w1_ref, w3_ref, w2_ref, *rest, mode, n_cast):
    if mode == "mid":
        gain2_ref, sh2_ref, sc2_ref = rest[:3]
        f32_refs, (h_ref, u2_ref) = rest[3:3 + n_cast], rest[3 + n_cast:5 + n_cast]
        bf16_refs, (u_scr, acc_scr) = rest[5 + n_cast:5 + 2 * n_cast], rest[5 + 2 * n_cast:]
    else:
        gainf_ref = rest[0]
        f32_refs, y_ref = rest[1:1 + n_cast], rest[1 + n_cast]
        bf16_refs, (u_scr, acc_scr) = rest[2 + n_cast:2 + 2 * n_cast], rest[2 + 2 * n_cast:]
    for src, dst in zip(f32_refs, bf16_refs):
        dst[...] = src[...].astype(BF16)
    f = pl.program_id(2)
    nf = pl.num_programs(2)
    def chunks(sizes):
        starts = [sum(sizes[:k]) for k in range(len(sizes))]
        return [slice(r, r + n) for r, n in zip(starts, sizes)]

    def swiglu_part(u):
        a = jnp.dot(u, w1_ref[...], preferred_element_type=F32)
        b = jnp.dot(u, w3_ref[...], preferred_element_type=F32)
        hm = (_silu(a) * b).astype(BF16)
        return jnp.dot(hm, w2_ref[...], preferred_element_type=F32)

    @pl.when(f == 0)
    def _():
        gain_mod = gain_ref[...] * (1.0 + sc_ref[0, 0])
        for rows in chunks(FFN_FIRST_ROWS):
            u = (_rms(x_ref[0, rows], gain_mod) + sh_ref[0, 0]).astype(BF16)
            u_scr[rows] = u
            acc_scr[rows] = swiglu_part(u)

    @pl.when((f > 0) & (f < nf - 1))
    def _():
        acc_scr[...] += swiglu_part(u_scr[...])

    @pl.when(f == nf - 1)
    def _():
        if mode == "mid":
            gain2_mod = gain2_ref[...] * (1.0 + sc2_ref[0, 0])
        for rows in chunks(FFN_LAST_ROWS):
            total = acc_scr[rows] + swiglu_part(u_scr[rows])
            h = x_ref[0, rows] + (0.5 * g_ref[0, 0]) * total
            if mode == "mid":
                h_ref[0, rows] = h
                u2_ref[0, rows] = (_rms(h, gain2_mod) + sh2_ref[0, 0]).astype(BF16)
            else:
                y_ref[0, rows] = _rms(h, gainf_ref[...])


def _mod_spec(k):
    return pl.BlockSpec((1, 1, 1, D_MODEL), lambda b, i, f, k=k: (b, k, 0, 0))


def _ffn_cast_spec(shape, n_b, n_i, n_f):
    rows = shape[0] // (n_b * n_i)
    assert rows * n_b * n_i == shape[0] and rows % BF16_SUBLANES == 0
    lane_blocks = shape[1] // LANES
    n_col = max(k for k in range(1, n_f + 1) if lane_blocks % k == 0)
    return pl.BlockSpec((rows, shape[1] // n_col),
                        lambda b, i, f: (b * n_i + i, jnp.minimum(f, n_col - 1)))


def _ffn(x, mod4, gain, w1, w3, w2, ks, mode, gain_next, ks_next=None, to_cast=()):
    n_b, s, d = x.shape
    tm, tf = FFN_TM, FFN_TF
    grid = (n_b, s // tm, D_FF // tf)
    cast_specs = [_ffn_cast_spec(arr.shape, *grid) for arr in to_cast]
    cast_shapes = [jax.ShapeDtypeStruct(arr.shape, BF16) for arr in to_cast]
    row_spec = pl.BlockSpec((1, tm, d), lambda b, i, f: (b, i, 0))
    vec_spec = pl.BlockSpec((1, d), lambda b, i, f: (0, 0))
    in_specs = [
        row_spec, vec_spec, _mod_spec(ks[0]), _mod_spec(ks[1]), _mod_spec(ks[2]),
        pl.BlockSpec((d, tf), lambda b, i, f: (0, f)),
        pl.BlockSpec((d, tf), lambda b, i, f: (0, f)),
        pl.BlockSpec((tf, d), lambda b, i, f: (f, 0)),
        vec_spec,
    ]
    args = [x, gain.reshape(1, d), mod4, mod4, mod4, w1, w3, w2, gain_next.reshape(1, d)]
    if mode == "mid":
        in_specs += [_mod_spec(ks_next[0]), _mod_spec(ks_next[1])]
        args += [mod4, mod4]
        out_specs = [row_spec, row_spec]
        out_shape = [jax.ShapeDtypeStruct((n_b, s, d), F32), jax.ShapeDtypeStruct((n_b, s, d), BF16)]
    else:
        out_specs = [row_spec]
        out_shape = [jax.ShapeDtypeStruct((n_b, s, d), F32)]
    return pl.pallas_call(
        functools.partial(_ffn_kernel, mode=mode, n_cast=len(to_cast)),
        grid=grid,
        in_specs=in_specs + cast_specs,
        out_specs=out_specs + cast_specs,
        out_shape=out_shape + cast_shapes,
        scratch_shapes=[pltpu.VMEM((tm, d), BF16), pltpu.VMEM((tm, d), F32)],
        compiler_params=_params(("arbitrary", "arbitrary", "arbitrary")),
        name="ffn_" + mode,
    )(*args, *to_cast)


PROJ_TM = 1024


def _proj_b_kernel(u_ref, w_ref, gain_ref, cos_ref, sin_ref, q_ref, k_ref, vt_ref, *, q_scale):
    acc = jnp.dot(u_ref[0], w_ref[...], preferred_element_type=F32)
    tm = acc.shape[0]
    for j in range(N_Q_B + N_KV_B):
        sl = slice(j * HEAD_DIM, (j + 1) * HEAD_DIM)
        y = _rms_head(acc[:, sl], gain_ref[:, sl])
        y = y * cos_ref[...] + pltpu.roll(y, HEAD_DIM // 2, axis=1) * sin_ref[...]
        if j < N_Q_B:
            q_ref[0, :, sl] = (y * q_scale).astype(BF16)
        else:
            k_ref[0, :, (j - N_Q_B) * HEAD_DIM:(j - N_Q_B + 1) * HEAD_DIM] = y.astype(BF16)
    v_t = acc[:, D_QB + D_KVB:].T.astype(BF16)
    for hh in range(N_KV_B):
        for cc in range(tm // GQA_TK):
            vt_ref[0, hh, cc] = v_t[hh * HEAD_DIM:(hh + 1) * HEAD_DIM, cc * GQA_TK:(cc + 1) * GQA_TK]


def _proj_b(u, w, col0, gain_row, cos, sin, q_scale):
    n_b, s, d = u.shape
    tm = PROJ_TM
    n = D_QB + 2 * D_KVB
    assert col0 % n == 0
    return pl.pallas_call(
        functools.partial(_proj_b_kernel, q_scale=q_scale),
        grid=(n_b, s // tm),
        in_specs=[
            pl.BlockSpec((1, tm, d), lambda b, i: (b, i, 0)),
            pl.BlockSpec((d, n), lambda b, i: (0, col0 // n)),
            pl.BlockSpec((1, D_QB + D_KVB), lambda b, i: (0, 0)),
            pl.BlockSpec((tm, HEAD_DIM), lambda b, i: (i, 0)),
            pl.BlockSpec((tm, HEAD_DIM), lambda b, i: (i, 0)),
        ],
        out_specs=[
            pl.BlockSpec((1, tm, D_QB), lambda b, i: (b, i, 0)),
            pl.BlockSpec((1, tm, D_KVB), lambda b, i: (b, i, 0)),
            pl.BlockSpec((1, N_KV_B, tm // GQA_TK, HEAD_DIM, GQA_TK), lambda b, i: (b, 0, i, 0, 0)),
        ],
        out_shape=[
            jax.ShapeDtypeStruct((n_b, s, D_QB), BF16),
            jax.ShapeDtypeStruct((n_b, s, D_KVB), BF16),
            jax.ShapeDtypeStruct((n_b, N_KV_B, s // GQA_TK, HEAD_DIM, GQA_TK), BF16),
        ],
        compiler_params=_params(("parallel", "arbitrary")),
        name="proj_mixer_b",
    )(u, w, gain_row, cos, sin)


def _proj_classes_kernel(u_ref, w_ref, *rest, norm, scale):
    if norm:
        gain_ref, rest = rest[0], rest[1:]
    f32_ref, rest = rest[0], rest[1:]
    o_refs, bf16_ref, scr = rest[:N_DIL_GROUPS], rest[N_DIL_GROUPS], rest[N_DIL_GROUPS + 1]
    bf16_ref[...] = f32_ref[...].astype(BF16)
    acc = jnp.dot(u_ref[0], w_ref[...], preferred_element_type=F32)
    tm = acc.shape[0]
    for g, (_, dil) in enumerate(DIL_GROUPS):
        for h in range(HEADS_PER_DIL):
            head = g * HEADS_PER_DIL + h
            lanes = slice(h * HEAD_DIM, (h + 1) * HEAD_DIM)
            y = acc[:, head * HEAD_DIM:(head + 1) * HEAD_DIM]
            if norm:
                y = _rms_head(y, gain_ref[...])
            if scale != 1.0:
                y = y * scale
            if dil == 1:
                o_refs[g][0, 0, :, lanes] = y.astype(BF16)
                continue
            scr[head] = y
            for r in range(dil):
                o_refs[g][0, r, :, lanes] = scr[head, pl.ds(r, tm // dil, stride=dil), :].astype(BF16)


def _proj_classes(u, w, which, to_cast, gain=None, scale=1.0):
    n_b, s, d = u.shape
    tm = PROJ_TM
    n_i = s // tm
    norm = gain is not None
    in_specs = [
        pl.BlockSpec((1, tm, d), lambda b, i: (b, i, 0)),
        pl.BlockSpec((d, D_A), lambda b, i: (0, which)),
    ]
    args = [u, w]
    if norm:
        in_specs.append(pl.BlockSpec((1, HEAD_DIM), lambda b, i: (0, 0)))
        args.append(gain.reshape(1, HEAD_DIM))
    cast_rows = to_cast.shape[0] // (n_b * n_i)
    assert cast_rows * n_b * n_i == to_cast.shape[0] and cast_rows % BF16_SUBLANES == 0
    cast_spec = pl.BlockSpec((cast_rows, to_cast.shape[1]), lambda b, i: (b * n_i + i, 0))
    out = pl.pallas_call(
        functools.partial(_proj_classes_kernel, norm=norm, scale=scale),
        grid=(n_b, n_i),
        in_specs=in_specs + [cast_spec],
        out_specs=[pl.BlockSpec((1, dil, tm // dil, D_A_OUT), lambda b, i: (b, 0, i, 0))
                   for _, dil in DIL_GROUPS] + [cast_spec],
        out_shape=[jax.ShapeDtypeStruct((n_b, dil, s // dil, D_A_OUT), BF16) for _, dil in DIL_GROUPS]
        + [jax.ShapeDtypeStruct(to_cast.shape, BF16)],
        scratch_shapes=[pltpu.VMEM((N_HEADS_A, tm, HEAD_DIM), F32)],
        compiler_params=_params(("parallel", "arbitrary")),
        name="proj_classes",
    )(*args, to_cast)
    return out[:N_DIL_GROUPS], out[N_DIL_GROUPS]


GQA_TQ = 512
GQA_TK = 512
LOG2E = 1.4426950408889634


def _gqa_kernel(q_ref, qn_ref, k_ref, vt_ref, o_ref, acc_ref, s_ref):
    n_chunks = vt_ref.shape[2]
    tq = q_ref.shape[1]
    heads = range(Q_PER_KV)
    nt_dims = (((1,), (1,)), ((), ()))

    def k_chunk(c):
        return k_ref[0, pl.ds(pl.multiple_of(c * GQA_TK, GQA_TK), GQA_TK), :]

    def q_head(g, ref=q_ref):
        return ref[0, :, g * HEAD_DIM:(g + 1) * HEAD_DIM]

    ones_rows = jnp.ones((BF16_SUBLANES, GQA_TK), BF16)

    def step(c, c_next, slot, ms, next_q_ref=q_ref):
        k_next = k_chunk(c_next)
        vt = jnp.concatenate([vt_ref[0, 0, c], ones_rows], axis=0)
        new_ms = []
        for g in heads:
            s_ref[1 - slot, g] = lax.dot_general(k_next, q_head(g, next_q_ref), nt_dims,
                                                 preferred_element_type=F32)
            m_new = jnp.maximum(ms[g], jnp.max(s_ref[slot, g], axis=0, keepdims=True))
            alpha = jnp.exp2(ms[g] - m_new)
            p = jnp.exp2(s_ref[slot, g] - m_new).astype(BF16)
            new_ms.append(m_new)
            acc_ref[g] = alpha * acc_ref[g] + jnp.dot(vt, p, preferred_element_type=F32)
        return tuple(new_ms)

    acc_ref[...] = jnp.zeros(acc_ref.shape, F32)

    @pl.when(pl.program_id(2) == 0)
    def _():
        k0 = k_chunk(0)
        for g in heads:
            s_ref[0, g] = lax.dot_general(k0, q_head(g), nt_dims, preferred_element_type=F32)

    def body(i, ms):
        ms = step(2 * i, 2 * i + 1, 0, ms)
        return step(2 * i + 1, 2 * i + 2, 1, ms)

    init = tuple(jnp.full((1, tq), MASK_VALUE, F32) for _ in heads)
    ms = lax.fori_loop(0, n_chunks // 2 - 1, body, init)
    ms = step(n_chunks - 2, n_chunks - 1, 0, ms)
    step(n_chunks - 1, 0, 1, ms, qn_ref)
    for g in heads:
        o_t = acc_ref[g, :HEAD_DIM] / acc_ref[g, HEAD_DIM:HEAD_DIM + 1]
        o_ref[0, :, g * HEAD_DIM:(g + 1) * HEAD_DIM] = o_t.T.astype(BF16)


def _gqa(qb, kb, vbt):
    n_b, s, _ = qb.shape
    tq = GQA_TQ
    w = Q_PER_KV * HEAD_DIM
    nq = s // tq
    return pl.pallas_call(
        _gqa_kernel,
        grid=(n_b, N_KV_B, nq),
        in_specs=[
            pl.BlockSpec((1, tq, w), lambda b, h, i: (b, i, h)),
            pl.BlockSpec((1, tq, w), lambda b, h, i: (b, jnp.minimum(i + 1, nq - 1), h)),
            pl.BlockSpec((1, s, HEAD_DIM), lambda b, h, i: (b, 0, h)),
            pl.BlockSpec((1, 1, s // GQA_TK, HEAD_DIM, GQA_TK), lambda b, h, i: (b, h, 0, 0, 0)),
        ],
        out_specs=pl.BlockSpec((1, tq, w), lambda b, h, i: (b, i, h)),
        out_shape=jax.ShapeDtypeStruct((n_b, s, D_QB), BF16),
        scratch_shapes=[pltpu.VMEM((Q_PER_KV, HEAD_DIM + BF16_SUBLANES, tq), F32),
                        pltpu.VMEM((2, Q_PER_KV, GQA_TK, tq), F32)],
        compiler_params=_params(("arbitrary", "arbitrary", "arbitrary")),
        name="gqa_attn",
    )(qb, qb, kb, vbt)


DIL_TQ = 512
DIL_SUB = 128
DIL_HALF = 64


def _dil_tile(q_ref, kp_ref, kc_ref, kn_ref, vp_ref, vc_ref, vn_ref, o_ref, lse_ref, i, heads, *,
              group, dil, n_rows):
    win = DIL_SUB + 2 * DIL_HALF
    row = lax.broadcasted_iota(jnp.int32, (DIL_SUB, win), 0)
    col = lax.broadcasted_iota(jnp.int32, (DIL_SUB, win), 1)
    dist = jnp.abs(col - DIL_HALF - row)
    in_band = dist <= DIL_HALF
    dist_f = dist.astype(F32) * float(dil)
    for h in heads:
        sl = slice(h * HEAD_DIM, (h + 1) * HEAD_DIM)
        slope = 2.0 ** (-ALIBI_MAX * (group * HEADS_PER_DIL + h + 1) / N_HEADS_A)
        k = jnp.concatenate([kp_ref[:, sl], kc_ref[:, sl], kn_ref[:, sl]], axis=0)
        v = jnp.concatenate([vp_ref[:, sl], vc_ref[:, sl], vn_ref[:, sl]], axis=0)
        for j in range(DIL_TQ // DIL_SUB):
            rows = slice(j * DIL_SUB, (j + 1) * DIL_SUB)
            k_abs = i * DIL_TQ + (j * DIL_SUB - DIL_HALF) + col
            valid = in_band & (k_abs >= 0) & (k_abs < n_rows)
            s = lax.dot_general(q_ref[rows, sl], k[j * DIL_SUB:j * DIL_SUB + win],
                                (((1,), (1,)), ((), ())), preferred_element_type=F32)
            s = jnp.where(valid, s - slope * dist_f, MASK_VALUE)
            m = jnp.max(s, axis=-1, keepdims=True)
            p = jnp.exp(s - m)
            l = jnp.sum(p, axis=-1, keepdims=True)
            o = jnp.dot(p.astype(BF16), v[j * DIL_SUB:j * DIL_SUB + win], preferred_element_type=F32)
            o_ref[rows, sl] = (o / l).astype(BF16)
            lse_ref[rows, sl] = jnp.broadcast_to(m + jnp.log(l), (DIL_SUB, HEAD_DIM))


GATES_TN = 2048
DIL_REFS = 7


def _gates_dilated_kernel(u_ref, w_ref, *rest, seq_len):
    n_in = DIL_REFS * N_DIL_GROUPS
    dil_in, gates_ref, dil_out = rest[:n_in], rest[n_in], rest[n_in + 1:]
    t = pl.program_id(1) * pl.num_programs(2) + pl.program_id(2)
    rows_per_head = u_ref.shape[1] // HEADS_PER_DIL
    for h in range(HEADS_PER_DIL):
        rows = slice(h * rows_per_head, (h + 1) * rows_per_head)
        half = w_ref.shape[1] // 2
        for g, (_, dil) in enumerate(DIL_GROUPS):
            n_rows = seq_len // dil
            _dil_tile(*dil_in[DIL_REFS * g:DIL_REFS * (g + 1)], dil_out[2 * g], dil_out[2 * g + 1],
                      t % (n_rows // DIL_TQ), (h,), group=g, dil=dil, n_rows=n_rows)
            if g < 2:
                cols = slice(g * half, (g + 1) * half)
                gates_ref[0, rows, cols] = jax.nn.sigmoid(
                    jnp.dot(u_ref[0, rows], w_ref[:, cols], preferred_element_type=F32)).astype(BF16)


def _gates_and_dilated(u, w, col0, qa, ka, va):
    n_b, s, d = u.shape
    tm, tn = PROJ_TM, GATES_TN
    n_i, n_j = s // tm, 2 * d // tn
    assert col0 % tn == 0
    in_specs = [
        pl.BlockSpec((1, tm, d), lambda b, j, i: (b, i, 0)),
        pl.BlockSpec((d, tn), lambda b, j, i: (0, col0 // tn + j), pipeline_mode=pl.Buffered(1)),
    ]
    args = [u, w]
    out_specs = [pl.BlockSpec((1, tm, tn), lambda b, j, i: (b, i, j))]
    out_shape = [jax.ShapeDtypeStruct((n_b, s, 2 * d), BF16)]
    for g, (_, dil) in enumerate(DIL_GROUPS):
        n_rows = s // dil
        per_class = n_rows // DIL_TQ
        assert dil * per_class == n_j * n_i
        halo_per_blk = DIL_TQ // DIL_HALF
        n_halo = n_rows // DIL_HALF

        def tile(j, i, per_class=per_class):
            t = j * n_i + i
            return t // per_class, t % per_class

        def cur_map(b, j, i, tile=tile):
            r, k = tile(j, i)
            return (b, r, k, 0)

        def prev_map(b, j, i, tile=tile, hpb=halo_per_blk):
            r, k = tile(j, i)
            return (b, r, jnp.maximum(k * hpb - 1, 0), 0)

        def next_map(b, j, i, tile=tile, hpb=halo_per_blk, n_halo=n_halo):
            r, k = tile(j, i)
            return (b, r, jnp.minimum((k + 1) * hpb, n_halo - 1), 0)

        cur = pl.BlockSpec((None, None, DIL_TQ, D_A_OUT), cur_map)
        prev = pl.BlockSpec((None, None, DIL_HALF, D_A_OUT), prev_map)
        nxt = pl.BlockSpec((None, None, DIL_HALF, D_A_OUT), next_map)
        in_specs += [cur, prev, cur, nxt, prev, cur, nxt]
        args += [qa[g], ka[g], ka[g], ka[g], va[g], va[g], va[g]]
        out_specs += [cur, cur]
        out_shape += [jax.ShapeDtypeStruct((n_b, dil, n_rows, D_A_OUT), BF16),
                      jax.ShapeDtypeStruct((n_b, dil, n_rows, D_A_OUT), F32)]
    out = pl.pallas_call(
        functools.partial(_gates_dilated_kernel, seq_len=s),
        grid=(n_b, n_j, n_i),
        in_specs=in_specs,
        out_specs=out_specs,
        out_shape=out_shape,
        compiler_params=_params(("arbitrary", "arbitrary", "arbitrary")),
        name="gates_dilated",
    )(*args)
    return out[0], out[1::2], out[2::2]


MERGE_TM = 256


def _merge_kernel(o0_ref, o1_ref, o2_ref, l0_ref, l1_ref, l2_ref, ob_ref, sa_ref, sb_ref,
                  wa_ref, wb_ref, wo_ref, h_ref, g_ref, out_ref, nat_ref):
    tm = out_ref.shape[1]

    def natural(ref, dil, slot):
        if dil == 1:
            return ref[0].astype(F32)
        for r in range(dil):
            for h in range(HEADS_PER_DIL):
                nat_ref[slot, h, pl.ds(r, tm // dil, stride=dil), :] = (
                    ref[r, :, h * HEAD_DIM:(h + 1) * HEAD_DIM].astype(F32))
        return jnp.concatenate([nat_ref[slot, h] for h in range(HEADS_PER_DIL)], axis=1)

    gated_b = sb_ref[0].astype(F32) * jnp.dot(ob_ref[0], wb_ref[...], preferred_element_type=F32)
    dils = [dil for _, dil in DIL_GROUPS]
    l0, l1, l2 = [natural(ref, dil, n) for n, (ref, dil) in enumerate(zip((l0_ref, l1_ref, l2_ref), dils))]
    o0, o1, o2 = [natural(ref, dil, N_DIL_GROUPS + n)
                  for n, (ref, dil) in enumerate(zip((o0_ref, o1_ref, o2_ref), dils))]
    mx = jnp.maximum(jnp.maximum(l0, l1), l2)
    e0, e1, e2 = jnp.exp(l0 - mx), jnp.exp(l1 - mx), jnp.exp(l2 - mx)
    oa = (e0 * o0 + e1 * o1 + e2 * o2) / (e0 + e1 + e2)
    pa = jnp.dot(oa.astype(BF16), wa_ref[...], preferred_element_type=F32)
    merged = sa_ref[0].astype(F32) * pa + gated_b
    mixed = jnp.dot(merged.astype(BF16), wo_ref[...], preferred_element_type=F32)
    out_ref[0] = h_ref[0] + g_ref[0, 0] * mixed


def _merge(o_groups, lse_groups, out_b, gates, wa, wb, wo, h, mod4, k_gate):
    n_b, s, d = h.shape
    tm = MERGE_TM
    a_specs = [pl.BlockSpec((None, dil, tm // dil, D_A_OUT), lambda b, i: (b, 0, i, 0))
               for _, dil in DIL_GROUPS]
    row_spec = pl.BlockSpec((1, tm, d), lambda b, i: (b, i, 0))
    full = lambda shape: pl.BlockSpec(shape, lambda b, i: (0, 0))
    return pl.pallas_call(
        _merge_kernel,
        grid=(n_b, s // tm),
        in_specs=a_specs + a_specs + [
            pl.BlockSpec((1, tm, D_QB), lambda b, i: (b, i, 0)),
            pl.BlockSpec((1, tm, d), lambda b, i: (b, i, 0)),
            pl.BlockSpec((1, tm, d), lambda b, i: (b, i, 1)),
            full((D_A_OUT, d)), full((D_QB, d)), full((d, d)),
            row_spec,
            pl.BlockSpec((1, 1, 1, d), lambda b, i: (b, k_gate, 0, 0)),
        ],
        out_specs=row_spec,
        out_shape=jax.ShapeDtypeStruct((n_b, s, d), F32),
        scratch_shapes=[pltpu.VMEM((2 * N_DIL_GROUPS, HEADS_PER_DIL, tm, HEAD_DIM), F32)],
        compiler_params=_params(("parallel", "arbitrary")),
        name="mix_merge",
    )(*o_groups, *lse_groups, out_b, gates, gates, wa, wb, wo, h, mod4)


def _rope_tables(s):
    rows = s // GRID_W
    half = HEAD_DIM // 2
    inv_freq = ROPE_THETA ** (-jnp.arange(0, half, 2, dtype=F32) / half)
    ang_row = jnp.arange(rows).astype(F32)[:, None] * inv_freq
    ang_col = jnp.arange(GRID_W).astype(F32)[:, None] * inv_freq

    def expand(fn):
        return jnp.concatenate([jnp.repeat(fn(ang_row), GRID_W, axis=0),
                                jnp.tile(fn(ang_col), (rows, 1))], axis=-1)

    cos, sin = expand(jnp.cos), expand(jnp.sin)
    return jnp.concatenate([cos, cos], axis=-1), jnp.concatenate([-sin, sin], axis=-1)


def kernel(x, c, w_ada, b_ada, norm_ffn1, w1_ffn1, w3_ffn1, w2_ffn1, norm_mix, w_in, q_norm_a, k_norm_a, q_norm_b, k_norm_b, w_branch_a, w_branch_b, w_out, norm_ffn2, w1_ffn2, w3_ffn2, w2_ffn2, norm_final):
    n_b, s, d = x.shape
    assert w_ada.shape[0] == 1, "the fused pipeline covers a single layer"
    cos_f, sin_f = _rope_tables(s)
    h = x
    for l in range(1):
        mod4 = _ada(c, w_ada[l], b_ada[l]).reshape(n_b, N_MOD, 1, d)
        h, u, wi, wa, wb, wo = _ffn(
            h, mod4, norm_ffn1[l], w1_ffn1[l].astype(BF16), w3_ffn1[l].astype(BF16),
            w2_ffn1[l].astype(BF16), (0, 1, 2), "mid", norm_mix[l], (3, 4),
            to_cast=[w_in[l], w_branch_a[l], w_branch_b[l], w_out[l]])
        o_qb = 3 * D_A
        o_gates = o_qb + D_QB + 2 * D_KVB
        gain_b = jnp.concatenate([jnp.tile(q_norm_b[l], N_Q_B), jnp.tile(k_norm_b[l], N_KV_B)])[None]
        qa, w1b = _proj_classes(u, wi, 0, w1_ffn2[l], q_norm_a[l], scale=SM_SCALE)
        ka, w3b = _proj_classes(u, wi, 1, w3_ffn2[l], k_norm_a[l])
        va, w2b = _proj_classes(u, wi, 2, w2_ffn2[l])
        qb, kb, vb = _proj_b(u, wi, o_qb, gain_b, cos_f, sin_f, SM_SCALE * LOG2E)
        gates, o_groups, lse_groups = _gates_and_dilated(u, wi, o_gates, qa, ka, va)
        out_b = _gqa(qb, kb, vb)
        h = _merge(o_groups, lse_groups, out_b, gates, wa, wb, wo, h, mod4, 5)
        (h,) = _ffn(h, mod4, norm_ffn2[l], w1b, w3b, w2b, (6, 7, 8), "final", norm_final)
    return h
```

```python
import functools

import jax
import jax.numpy as jnp
from jax import lax
from jax.experimental import pallas as pl
from jax.experimental.pallas import tpu as pltpu

F32 = jnp.float32
BF16 = jnp.bfloat16

D_MODEL = 2048
D_FF = 5632
HEAD_DIM = 128
DIL_GROUPS = ((128, 1), (512, 4), (2048, 16))
N_DIL_GROUPS = 3
HEADS_PER_DIL = 4
N_HEADS_A = N_DIL_GROUPS * HEADS_PER_DIL
N_Q_B = 8
N_KV_B = 2
Q_PER_KV = N_Q_B // N_KV_B
GRID_W = 64
ROPE_THETA = 10000.0
ALIBI_MAX = 8.0
N_MOD = 9
EPS = 1e-6
D_A = N_HEADS_A * HEAD_DIM
D_A_OUT = HEADS_PER_DIL * HEAD_DIM
D_QB = N_Q_B * HEAD_DIM
D_KVB = N_KV_B * HEAD_DIM
SM_SCALE = HEAD_DIM ** -0.5
MASK_VALUE = -1e30

LANES = 128
BF16_SUBLANES = 16
VMEM_LIMIT = 56 * 1024 * 1024


def _params(sem):
    return pltpu.CompilerParams(dimension_semantics=sem, vmem_limit_bytes=VMEM_LIMIT)


def _silu(x):
    return x * jax.nn.sigmoid(x)


def _rms(x, gain):
    ms = jnp.mean(x * x, axis=-1, keepdims=True)
    return x * lax.rsqrt(ms + EPS) * gain


def _rms_head(x, gain):
    sq = x * x
    hi = sq.astype(BF16)
    lo = (sq - hi.astype(F32)).astype(BF16)
    ones = jnp.ones((HEAD_DIM, HEAD_DIM), BF16)
    ssq = (jnp.dot(hi, ones, preferred_element_type=F32) + jnp.dot(lo, ones, preferred_element_type=F32))
    return x * lax.rsqrt(ssq * (1.0 / HEAD_DIM) + EPS) * gain


ADA_TN = 2048


def _ada_kernel(c_ref, w_ref, b_ref, o_ref):
    n_b = c_ref.shape[0]
    for b in range(n_b):
        cs = _silu(c_ref[b])
        for j in range(ADA_TN // LANES):
            sl = slice(j * LANES, (j + 1) * LANES)
            acc = jnp.sum(w_ref[:, sl] * cs, axis=0, keepdims=True)
            o_ref[b, :, sl] = acc + b_ref[:, sl]


def _ada(c, w, b):
    n_b, d = c.shape
    n = w.shape[1]
    c_b = jnp.broadcast_to(c[:, :, None], (n_b, d, LANES))
    return pl.pallas_call(
        _ada_kernel,
        grid=(n // ADA_TN,),
        in_specs=[
            pl.BlockSpec((n_b, d, LANES), lambda j: (0, 0, 0)),
            pl.BlockSpec((d, ADA_TN), lambda j: (0, j)),
            pl.BlockSpec((1, ADA_TN), lambda j: (0, j)),
        ],
        out_specs=pl.BlockSpec((n_b, 1, ADA_TN), lambda j: (0, 0, j)),
        out_shape=jax.ShapeDtypeStruct((n_b, 1, n), F32),
        compiler_params=_params(("arbitrary",)),
        name="ada_mod",
    )(c_b, w, b.reshape(1, n))


FFN_TM = 512
FFN_TF = 512
FFN_WEIGHT_BUFFERS = 3
FFN_FIRST_ROWS = (256, 256)
FFN_LAST_ROWS = (256, 256)


def _ffn_kernel(x_ref, gain_ref, sh_ref, sc_ref, g_ref, w1_hbm, w3_hbm, w2_hbm, *rest, mode, n_cast):
    u_scr, acc_scr, w1_buf, w3_buf, w2_buf, w_sem = rest[-6:]
    rest = rest[:-6]
    if mode == "mid":
        gain2_ref, sh2_ref, sc2_ref = rest[:3]
        f32_refs, (h_ref, u2_ref) = rest[3:3 + n_cast], rest[3 + n_cast:5 + n_cast]
        bf16_refs = rest[5 + n_cast:5 + 2 * n_cast]
    else:
        gainf_ref = rest[0]
        f32_refs, y_ref = rest[1:1 + n_cast], rest[1 + n_cast]
        bf16_refs = rest[2 + n_cast:2 + 2 * n_cast]
    f = pl.program_id(2)
    nf = pl.num_programs(2)
    tf = w1_buf.shape[2]
    n_buf = w1_buf.shape[0]
    step = (pl.program_id(0) * pl.num_programs(1) + pl.program_id(1)) * nf + f
    n_steps = pl.num_programs(0) * pl.num_programs(1) * nf

    def weight_copies(st, slot):
        c0 = pl.multiple_of((st % nf) * tf, tf)
        return (pltpu.make_async_copy(w1_hbm.at[:, pl.ds(c0, tf)], w1_buf.at[slot], w_sem.at[0, slot]),
                pltpu.make_async_copy(w3_hbm.at[:, pl.ds(c0, tf)], w3_buf.at[slot], w_sem.at[1, slot]),
                pltpu.make_async_copy(w2_hbm.at[pl.ds(c0, tf), :], w2_buf.at[slot], w_sem.at[2, slot]))

    @pl.when(step == 0)
    def _():
        for st in range(n_buf - 1):
            for cp in weight_copies(st, st):
                cp.start()

    ahead = step + (n_buf - 1)

    @pl.when(ahead < n_steps)
    def _():
        for cp in weight_copies(ahead, ahead % n_buf):
            cp.start()

    slot = step % n_buf
    for cp in weight_copies(step, slot):
        cp.wait()
    w1_ref, w3_ref, w2_ref = w1_buf.at[slot], w3_buf.at[slot], w2_buf.at[slot]

    for src, dst in zip(f32_refs, bf16_refs):
        dst[...] = src[...].astype(BF16)

    def chunks(sizes):
        starts = [sum(sizes[:k]) for k in range(len(sizes))]
        return [slice(r, r + n) for r, n in zip(starts, sizes)]

    def swiglu_part(u):
        a = jnp.dot(u, w1_ref[...], preferred_element_type=F32)
        b = jnp.dot(u, w3_ref[...], preferred_element_type=F32)
        hm = (_silu(a) * b).astype(BF16)
        return jnp.dot(hm, w2_ref[...], preferred_element_type=F32)

    @pl.when(f == 0)
    def _():
        gain_mod = gain_ref[...] * (1.0 + sc_ref[0, 0])
        for rows in chunks(FFN_FIRST_ROWS):
            u = (_rms(x_ref[0, rows], gain_mod) + sh_ref[0, 0]).astype(BF16)
            u_scr[rows] = u
            acc_scr[rows] = swiglu_part(u)

    @pl.when((f > 0) & (f < nf - 1))
    def _():
        acc_scr[...] += swiglu_part(u_scr[...])

    @pl.when(f == nf - 1)
    def _():
        if mode == "mid":
            gain2_mod = gain2_ref[...] * (1.0 + sc2_ref[0, 0])
        for rows in chunks(FFN_LAST_ROWS):
            total = acc_scr[rows] + swiglu_part(u_scr[rows])
            h = x_ref[0, rows] + (0.5 * g_ref[0, 0]) * total
            if mode == "mid":
                h_ref[0, rows] = h
                u2_ref[0, rows] = (_rms(h, gain2_mod) + sh2_ref[0, 0]).astype(BF16)
            else:
                y_ref[0, rows] = _rms(h, gainf_ref[...])


def _mod_spec(k):
    return pl.BlockSpec((1, 1, 1, D_MODEL), lambda b, i, f, k=k: (b, k, 0, 0))


def _ffn_cast_spec(shape, n_b, n_i, n_f):
    rows = shape[0] // (n_b * n_i)
    assert rows * n_b * n_i == shape[0] and rows % BF16_SUBLANES == 0
    lane_blocks = shape[1] // LANES
    n_col = max(k for k in range(1, n_f + 1) if lane_blocks % k == 0)
    return pl.BlockSpec((rows, shape[1] // n_col),
                        lambda b, i, f: (b * n_i + i, jnp.minimum(f, n_col - 1)))


def _ffn(x, mod4, gain, w1, w3, w2, ks, mode, gain_next, ks_next=None, to_cast=()):
    n_b, s, d = x.shape
    tm, tf = FFN_TM, FFN_TF
    grid = (n_b, s // tm, D_FF // tf)
    cast_specs = [_ffn_cast_spec(arr.shape, *grid) for arr in to_cast]
    cast_shapes = [jax.ShapeDtypeStruct(arr.shape, BF16) for arr in to_cast]
    row_spec = pl.BlockSpec((1, tm, d), lambda b, i, f: (b, i, 0))
    vec_spec = pl.BlockSpec((1, d), lambda b, i, f: (0, 0))
    in_specs = [
        row_spec, vec_spec, _mod_spec(ks[0]), _mod_spec(ks[1]), _mod_spec(ks[2]),
        pl.BlockSpec(memory_space=pl.ANY),
        pl.BlockSpec(memory_space=pl.ANY),
        pl.BlockSpec(memory_space=pl.ANY),
        vec_spec,
    ]
    args = [x, gain.reshape(1, d), mod4, mod4, mod4, w1, w3, w2, gain_next.reshape(1, d)]
    if mode == "mid":
        in_specs += [_mod_spec(ks_next[0]), _mod_spec(ks_next[1])]
        args += [mod4, mod4]
        out_specs = [row_spec, row_spec]
        out_shape = [jax.ShapeDtypeStruct((n_b, s, d), F32), jax.ShapeDtypeStruct((n_b, s, d), BF16)]
    else:
        out_specs = [row_spec]
        out_shape = [jax.ShapeDtypeStruct((n_b, s, d), F32)]
    return pl.pallas_call(
        functools.partial(_ffn_kernel, mode=mode, n_cast=len(to_cast)),
        grid=grid,
        in_specs=in_specs + cast_specs,
        out_specs=out_specs + cast_specs,
        out_shape=out_shape + cast_shapes,
        scratch_shapes=[pltpu.VMEM((tm, d), BF16), pltpu.VMEM((tm, d), F32),
                        pltpu.VMEM((FFN_WEIGHT_BUFFERS, d, tf), BF16),
                        pltpu.VMEM((FFN_WEIGHT_BUFFERS, d, tf), BF16),
                        pltpu.VMEM((FFN_WEIGHT_BUFFERS, tf, d), BF16),
                        pltpu.SemaphoreType.DMA((3, FFN_WEIGHT_BUFFERS))],
        compiler_params=_params(("arbitrary", "arbitrary", "arbitrary")),
        name="ffn_" + mode,
    )(*args, *to_cast)


PROJ_TM = 1024


def _proj_b_kernel(u_ref, w_ref, gain_ref, cos_ref, sin_ref, q_ref, k_ref, vt_ref, *, q_scale):
    acc = jnp.dot(u_ref[0], w_ref[...], preferred_element_type=F32)
    tm = acc.shape[0]
    for j in range(N_Q_B + N_KV_B):
        sl = slice(j * HEAD_DIM, (j + 1) * HEAD_DIM)
        y = _rms_head(acc[:, sl], gain_ref[:, sl])
        y = y * cos_ref[...] + pltpu.roll(y, HEAD_DIM // 2, axis=1) * sin_ref[...]
        if j < N_Q_B:
            q_ref[0, :, sl] = (y * q_scale).astype(BF16)
        else:
            k_ref[0, :, (j - N_Q_B) * HEAD_DIM:(j - N_Q_B + 1) * HEAD_DIM] = y.astype(BF16)
    v_t = acc[:, D_QB + D_KVB:].T.astype(BF16)
    for hh in range(N_KV_B):
        for cc in range(tm // GQA_TK):
            vt_ref[0, hh, cc] = v_t[hh * HEAD_DIM:(hh + 1) * HEAD_DIM, cc * GQA_TK:(cc + 1) * GQA_TK]


def _proj_b(u, w, col0, gain_row, cos, sin, q_scale):
    n_b, s, d = u.shape
    tm = PROJ_TM
    n = D_QB + 2 * D_KVB
    assert col0 % n == 0
    return pl.pallas_call(
        functools.partial(_proj_b_kernel, q_scale=q_scale),
        grid=(n_b, s // tm),
        in_specs=[
            pl.BlockSpec((1, tm, d), lambda b, i: (b, i, 0)),
            pl.BlockSpec((d, n), lambda b, i: (0, col0 // n)),
            pl.BlockSpec((1, D_QB + D_KVB), lambda b, i: (0, 0)),
            pl.BlockSpec((tm, HEAD_DIM), lambda b, i: (i, 0)),
            pl.BlockSpec((tm, HEAD_DIM), lambda b, i: (i, 0)),
        ],
        out_specs=[
            pl.BlockSpec((1, tm, D_QB), lambda b, i: (b, i, 0)),
            pl.BlockSpec((1, tm, D_KVB), lambda b, i: (b, i, 0)),
            pl.BlockSpec((1, N_KV_B, tm // GQA_TK, HEAD_DIM, GQA_TK), lambda b, i: (b, 0, i, 0, 0)),
        ],
        out_shape=[
            jax.ShapeDtypeStruct((n_b, s, D_QB), BF16),
            jax.ShapeDtypeStruct((n_b, s, D_KVB), BF16),
            jax.ShapeDtypeStruct((n_b, N_KV_B, s // GQA_TK, HEAD_DIM, GQA_TK), BF16),
        ],
        compiler_params=_params(("parallel", "arbitrary")),
        name="proj_mixer_b",
    )(u, w, gain_row, cos, sin)


def _proj_classes_kernel(u_ref, w_ref, *rest, norm, scale):
    if norm:
        gain_ref, rest = rest[0], rest[1:]
    f32_ref, rest = rest[0], rest[1:]
    o_refs, bf16_ref, scr = rest[:N_DIL_GROUPS], rest[N_DIL_GROUPS], rest[N_DIL_GROUPS + 1]
    bf16_ref[...] = f32_ref[...].astype(BF16)
    acc = jnp.dot(u_ref[0], w_ref[...], preferred_element_type=F32)
    tm = acc.shape[0]
    for g, (_, dil) in enumerate(DIL_GROUPS):
        for h in range(HEADS_PER_DIL):
            head = g * HEADS_PER_DIL + h
            lanes = slice(h * HEAD_DIM, (h + 1) * HEAD_DIM)
            y = acc[:, head * HEAD_DIM:(head + 1) * HEAD_DIM]
            if norm:
                y = _rms_head(y, gain_ref[...])
            if scale != 1.0:
                y = y * scale
            if dil == 1:
                o_refs[g][0, 0, :, lanes] = y.astype(BF16)
                continue
            scr[head] = y
            for r in range(dil):
                o_refs[g][0, r, :, lanes] = scr[head, pl.ds(r, tm // dil, stride=dil), :].astype(BF16)


def _proj_classes(u, w, which, to_cast, gain=None, scale=1.0):
    n_b, s, d = u.shape
    tm = PROJ_TM
    n_i = s // tm
    norm = gain is not None
    in_specs = [
        pl.BlockSpec((1, tm, d), lambda b, i: (b, i, 0)),
        pl.BlockSpec((d, D_A), lambda b, i: (0, which)),
    ]
    args = [u, w]
    if norm:
        in_specs.append(pl.BlockSpec((1, HEAD_DIM), lambda b, i: (0, 0)))
        args.append(gain.reshape(1, HEAD_DIM))
    cast_rows = to_cast.shape[0] // (n_b * n_i)
    assert cast_rows * n_b * n_i == to_cast.shape[0] and cast_rows % BF16_SUBLANES == 0
    cast_spec = pl.BlockSpec((cast_rows, to_cast.shape[1]), lambda b, i: (b * n_i + i, 0))
    out = pl.pallas_call(
        functools.partial(_proj_classes_kernel, norm=norm, scale=scale),
        grid=(n_b, n_i),
        in_specs=in_specs + [cast_spec],
        out_specs=[pl.BlockSpec((1, dil, tm // dil, D_A_OUT), lambda b, i: (b, 0, i, 0))
                   for _, dil in DIL_GROUPS] + [cast_spec],
        out_shape=[jax.ShapeDtypeStruct((n_b, dil, s // dil, D_A_OUT), BF16) for _, dil in DIL_GROUPS]
        + [jax.ShapeDtypeStruct(to_cast.shape, BF16)],
        scratch_shapes=[pltpu.VMEM((N_HEADS_A, tm, HEAD_DIM), F32)],
        compiler_params=_params(("parallel", "arbitrary")),
        name="proj_classes",
    )(*args, to_cast)
    return out[:N_DIL_GROUPS], out[N_DIL_GROUPS]


GQA_TQ = 512
GQA_TK = 512
LOG2E = 1.4426950408889634


def _gqa_kernel(q_ref, qn_ref, k_ref, vt_ref, o_ref, acc_ref, s_ref):
    n_chunks = vt_ref.shape[2]
    tq = q_ref.shape[1]
    heads = range(Q_PER_KV)
    nt_dims = (((1,), (1,)), ((), ()))

    def k_chunk(c):
        return k_ref[0, pl.ds(pl.multiple_of(c * GQA_TK, GQA_TK), GQA_TK), :]

    def q_head(g, ref=q_ref):
        return ref[0, :, g * HEAD_DIM:(g + 1) * HEAD_DIM]

    ones_rows = jnp.ones((BF16_SUBLANES, GQA_TK), BF16)

    def step(c, c_next, slot, ms, next_q_ref=q_ref):
        k_next = k_chunk(c_next)
        vt = jnp.concatenate([vt_ref[0, 0, c], ones_rows], axis=0)
        new_ms = []
        for g in heads:
            s_ref[1 - slot, g] = lax.dot_general(k_next, q_head(g, next_q_ref), nt_dims,
                                                 preferred_element_type=F32)
            m_new = jnp.maximum(ms[g], jnp.max(s_ref[slot, g], axis=0, keepdims=True))
            alpha = jnp.exp2(ms[g] - m_new)
            p = jnp.exp2(s_ref[slot, g] - m_new).astype(BF16)
            new_ms.append(m_new)
            acc_ref[g] = alpha * acc_ref[g] + jnp.dot(vt, p, preferred_element_type=F32)
        return tuple(new_ms)

    acc_ref[...] = jnp.zeros(acc_ref.shape, F32)

    @pl.when(pl.program_id(2) == 0)
    def _():
        k0 = k_chunk(0)
        for g in heads:
            s_ref[0, g] = lax.dot_general(k0, q_head(g), nt_dims, preferred_element_type=F32)

    def body(i, ms):
        ms = step(2 * i, 2 * i + 1, 0, ms)
        return step(2 * i + 1, 2 * i + 2, 1, ms)

    init = tuple(jnp.full((1, tq), MASK_VALUE, F32) for _ in heads)
    ms = lax.fori_loop(0, n_chunks // 2 - 1, body, init)
    ms = step(n_chunks - 2, n_chunks - 1, 0, ms)
    step(n_chunks - 1, 0, 1, ms, qn_ref)
    for g in heads:
        o_t = acc_ref[g, :HEAD_DIM] / acc_ref[g, HEAD_DIM:HEAD_DIM + 1]
        o_ref[0, :, g * HEAD_DIM:(g + 1) * HEAD_DIM] = o_t.T.astype(BF16)


def _gqa(qb, kb, vbt):
    n_b, s, _ = qb.shape
    tq = GQA_TQ
    w = Q_PER_KV * HEAD_DIM
    nq = s // tq
    return pl.pallas_call(
        _gqa_kernel,
        grid=(n_b, N_KV_B, nq),
        in_specs=[
            pl.BlockSpec((1, tq, w), lambda b, h, i: (b, i, h)),
            pl.BlockSpec((1, tq, w), lambda b, h, i: (b, jnp.minimum(i + 1, nq - 1), h)),
            pl.BlockSpec((1, s, HEAD_DIM), lambda b, h, i: (b, 0, h)),
            pl.BlockSpec((1, 1, s // GQA_TK, HEAD_DIM, GQA_TK), lambda b, h, i: (b, h, 0, 0, 0)),
        ],
        out_specs=pl.BlockSpec((1, tq, w), lambda b, h, i: (b, i, h)),
        out_shape=jax.ShapeDtypeStruct((n_b, s, D_QB), BF16),
        scratch_shapes=[pltpu.VMEM((Q_PER_KV, HEAD_DIM + BF16_SUBLANES, tq), F32),
                        pltpu.VMEM((2, Q_PER_KV, GQA_TK, tq), F32)],
        compiler_params=_params(("arbitrary", "arbitrary", "arbitrary")),
        name="gqa_attn",
    )(qb, qb, kb, vbt)


DIL_TQ = 512
DIL_SUB = 128
DIL_HALF = 64


def _dil_tile(q_ref, kp_ref, kc_ref, kn_ref, vp_ref, vc_ref, vn_ref, o_ref, lse_ref, i, heads, *,
              group, dil, n_rows):
    win = DIL_SUB + 2 * DIL_HALF
    row = lax.broadcasted_iota(jnp.int32, (DIL_SUB, win), 0)
    col = lax.broadcasted_iota(jnp.int32, (DIL_SUB, win), 1)
    dist = jnp.abs(col - DIL_HALF - row)
    in_band = dist <= DIL_HALF
    dist_f = dist.astype(F32) * float(dil)
    for h in heads:
        sl = slice(h * HEAD_DIM, (h + 1) * HEAD_DIM)
        slope = 2.0 ** (-ALIBI_MAX * (group * HEADS_PER_DIL + h + 1) / N_HEADS_A)
        k = jnp.concatenate([kp_ref[:, sl], kc_ref[:, sl], kn_ref[:, sl]], axis=0)
        v = jnp.concatenate([vp_ref[:, sl], vc_ref[:, sl], vn_ref[:, sl]], axis=0)
        for j in range(DIL_TQ // DIL_SUB):
            rows = slice(j * DIL_SUB, (j + 1) * DIL_SUB)
            k_abs = i * DIL_TQ + (j * DIL_SUB - DIL_HALF) + col
            valid = in_band & (k_abs >= 0) & (k_abs < n_rows)
            s = lax.dot_general(q_ref[rows, sl], k[j * DIL_SUB:j * DIL_SUB + win],
                                (((1,), (1,)), ((), ())), preferred_element_type=F32)
            s = jnp.where(valid, s - slope * dist_f, MASK_VALUE)
            m = jnp.max(s, axis=-1, keepdims=True)
            p = jnp.exp(s - m)
            l = jnp.sum(p, axis=-1, keepdims=True)
            o = jnp.dot(p.astype(BF16), v[j * DIL_SUB:j * DIL_SUB + win], preferred_element_type=F32)
            o_ref[rows, sl] = (o / l).astype(BF16)
            lse_ref[rows, sl] = jnp.broadcast_to(m + jnp.log(l), (DIL_SUB, HEAD_DIM))


GATES_TN = 2048
DIL_REFS = 7


def _gates_dilated_kernel(u_ref, w_ref, *rest, seq_len):
    n_in = DIL_REFS * N_DIL_GROUPS
    dil_in, gates_ref, dil_out = rest[:n_in], rest[n_in], rest[n_in + 1:]
    t = pl.program_id(1) * pl.num_programs(2) + pl.program_id(2)
    rows_per_head = u_ref.shape[1] // HEADS_PER_DIL
    for h in range(HEADS_PER_DIL):
        rows = slice(h * rows_per_head, (h + 1) * rows_per_head)
        half = w_ref.shape[1] // 2
        for g, (_, dil) in enumerate(DIL_GROUPS):
            n_rows = seq_len // dil
            _dil_tile(*dil_in[DIL_REFS * g:DIL_REFS * (g + 1)], dil_out[2 * g], dil_out[2 * g + 1],
                      t % (n_rows // DIL_TQ), (h,), group=g, dil=dil, n_rows=n_rows)
            if g < 2:
                cols = slice(g * half, (g + 1) * half)
                gates_ref[0, rows, cols] = jax.nn.sigmoid(
                    jnp.dot(u_ref[0, rows], w_ref[:, cols], preferred_element_type=F32)).astype(BF16)


def _gates_and_dilated(u, w, col0, qa, ka, va):
    n_b, s, d = u.shape
    tm, tn = PROJ_TM, GATES_TN
    n_i, n_j = s // tm, 2 * d // tn
    assert col0 % tn == 0
    in_specs = [
        pl.BlockSpec((1, tm, d), lambda b, j, i: (b, i, 0)),
        pl.BlockSpec((d, tn), lambda b, j, i: (0, col0 // tn + j), pipeline_mode=pl.Buffered(1)),
    ]
    args = [u, w]
    out_specs = [pl.BlockSpec((1, tm, tn), lambda b, j, i: (b, i, j))]
    out_shape = [jax.ShapeDtypeStruct((n_b, s, 2 * d), BF16)]
    for g, (_, dil) in enumerate(DIL_GROUPS):
        n_rows = s // dil
        per_class = n_rows // DIL_TQ
        assert dil * per_class == n_j * n_i
        halo_per_blk = DIL_TQ // DIL_HALF
        n_halo = n_rows // DIL_HALF

        def tile(j, i, per_class=per_class):
            t = j * n_i + i
            return t // per_class, t % per_class

        def cur_map(b, j, i, tile=tile):
            r, k = tile(j, i)
            return (b, r, k, 0)

        def prev_map(b, j, i, tile=tile, hpb=halo_per_blk):
            r, k = tile(j, i)
            return (b, r, jnp.maximum(k * hpb - 1, 0), 0)

        def next_map(b, j, i, tile=tile, hpb=halo_per_blk, n_halo=n_halo):
            r, k = tile(j, i)
            return (b, r, jnp.minimum((k + 1) * hpb, n_halo - 1), 0)

        cur = pl.BlockSpec((None, None, DIL_TQ, D_A_OUT), cur_map)
        prev = pl.BlockSpec((None, None, DIL_HALF, D_A_OUT), prev_map)
        nxt = pl.BlockSpec((None, None, DIL_HALF, D_A_OUT), next_map)
        in_specs += [cur, prev, cur, nxt, prev, cur, nxt]
        args += [qa[g], ka[g], ka[g], ka[g], va[g], va[g], va[g]]
        out_specs += [cur, cur]
        out_shape += [jax.ShapeDtypeStruct((n_b, dil, n_rows, D_A_OUT), BF16),
                      jax.ShapeDtypeStruct((n_b, dil, n_rows, D_A_OUT), F32)]
    out = pl.pallas_call(
        functools.partial(_gates_dilated_kernel, seq_len=s),
        grid=(n_b, n_j, n_i),
        in_specs=in_specs,
        out_specs=out_specs,
        out_shape=out_shape,
        compiler_params=_params(("arbitrary", "arbitrary", "arbitrary")),
        name="gates_dilated",
    )(*args)
    return out[0], out[1::2], out[2::2]


MERGE_TM = 256


def _merge_kernel(o0_ref, o1_ref, o2_ref, l0_ref, l1_ref, l2_ref, ob_ref, sa_ref, sb_ref,
                  wa_ref, wb_ref, wo_ref, h_ref, g_ref, out_ref, nat_ref):
    tm = out_ref.shape[1]

    def natural(ref, dil, slot):
        if dil == 1:
            return ref[0].astype(F32)
        for r in range(dil):
            for h in range(HEADS_PER_DIL):
                nat_ref[slot, h, pl.ds(r, tm // dil, stride=dil), :] = (
                    ref[r, :, h * HEAD_DIM:(h + 1) * HEAD_DIM].astype(F32))
        return jnp.concatenate([nat_ref[slot, h] for h in range(HEADS_PER_DIL)], axis=1)

    gated_b = sb_ref[0].astype(F32) * jnp.dot(ob_ref[0], wb_ref[...], preferred_element_type=F32)
    dils = [dil for _, dil in DIL_GROUPS]
    l0, l1, l2 = [natural(ref, dil, n) for n, (ref, dil) in enumerate(zip((l0_ref, l1_ref, l2_ref), dils))]
    o0, o1, o2 = [natural(ref, dil, N_DIL_GROUPS + n)
                  for n, (ref, dil) in enumerate(zip((o0_ref, o1_ref, o2_ref), dils))]
    mx = jnp.maximum(jnp.maximum(l0, l1), l2)
    e0, e1, e2 = jnp.exp(l0 - mx), jnp.exp(l1 - mx), jnp.exp(l2 - mx)
    oa = (e0 * o0 + e1 * o1 + e2 * o2) / (e0 + e1 + e2)
    pa = jnp.dot(oa.astype(BF16), wa_ref[...], preferred_element_type=F32)
    merged = sa_ref[0].astype(F32) * pa + gated_b
    mixed = jnp.dot(merged.astype(BF16), wo_ref[...], preferred_element_type=F32)
    out_ref[0] = h_ref[0] + g_ref[0, 0] * mixed


def _merge(o_groups, lse_groups, out_b, gates, wa, wb, wo, h, mod4, k_gate):
    n_b, s, d = h.shape
    tm = MERGE_TM
    a_specs = [pl.BlockSpec((None, dil, tm // dil, D_A_OUT), lambda b, i: (b, 0, i, 0))
               for _, dil in DIL_GROUPS]
    row_spec = pl.BlockSpec((1, tm, d), lambda b, i: (b, i, 0))
    full = lambda shape: pl.BlockSpec(shape, lambda b, i: (0, 0))
    return pl.pallas_call(
        _merge_kernel,
        grid=(n_b, s // tm),
        in_specs=a_specs + a_specs + [
            pl.BlockSpec((1, tm, D_QB), lambda b, i: (b, i, 0)),
            pl.BlockSpec((1, tm, d), lambda b, i: (b, i, 0)),
            pl.BlockSpec((1, tm, d), lambda b, i: (b, i, 1)),
            full((D_A_OUT, d)), full((D_QB, d)), full((d, d)),
            row_spec,
            pl.BlockSpec((1, 1, 1, d), lambda b, i: (b, k_gate, 0, 0)),
        ],
        out_specs=row_spec,
        out_shape=jax.ShapeDtypeStruct((n_b, s, d), F32),
        scratch_shapes=[pltpu.VMEM((2 * N_DIL_GROUPS, HEADS_PER_DIL, tm, HEAD_DIM), F32)],
        compiler_params=_params(("parallel", "arbitrary")),
        name="mix_merge",
    )(*o_groups, *lse_groups, out_b, gates, gates, wa, wb, wo, h, mod4)


def _rope_tables(s):
    rows = s // GRID_W
    half = HEAD_DIM // 2
    inv_freq = ROPE_THETA ** (-jnp.arange(0, half, 2, dtype=F32) / half)
    ang_row = jnp.arange(rows).astype(F32)[:, None] * inv_freq
    ang_col = jnp.arange(GRID_W).astype(F32)[:, None] * inv_freq

    def expand(fn):
        return jnp.concatenate([jnp.repeat(fn(ang_row), GRID_W, axis=0),
                                jnp.tile(fn(ang_col), (rows, 1))], axis=-1)

    cos, sin = expand(jnp.cos), expand(jnp.sin)
    return jnp.concatenate([cos, cos], axis=-1), jnp.concatenate([-sin, sin], axis=-1)


def kernel(x, c, w_ada, b_ada, norm_ffn1, w1_ffn1, w3_ffn1, w2_ffn1, norm_mix, w_in, q_norm_a, k_norm_a, q_norm_b, k_norm_b, w_branch_a, w_branch_b, w_out, norm_ffn2, w1_ffn2, w3_ffn2, w2_ffn2, norm_final):
    n_b, s, d = x.shape
    assert w_ada.shape[0] == 1, "the fused pipeline covers a single layer"
    cos_f, sin_f = _rope_tables(s)
    h = x
    for l in range(1):
        mod4 = _ada(c, w_ada[l], b_ada[l]).reshape(n_b, N_MOD, 1, d)
        h, u, wi, wa, wb, wo = _ffn(
            h, mod4, norm_ffn1[l], w1_ffn1[l].astype(BF16), w3_ffn1[l].astype(BF16),
            w2_ffn1[l].astype(BF16), (0, 1, 2), "mid", norm_mix[l], (3, 4),
            to_cast=[w_in[l], w_branch_a[l], w_branch_b[l], w_out[l]])
        o_qb = 3 * D_A
        o_gates = o_qb + D_QB + 2 * D_KVB
        gain_b = jnp.concatenate([jnp.tile(q_norm_b[l], N_Q_B), jnp.tile(k_norm_b[l], N_KV_B)])[None]
        qa, w1b = _proj_classes(u, wi, 0, w1_ffn2[l], q_norm_a[l], scale=SM_SCALE)
        ka, w3b = _proj_classes(u, wi, 1, w3_ffn2[l], k_norm_a[l])
        va, w2b = _proj_classes(u, wi, 2, w2_ffn2[l])
        qb, kb, vb = _proj_b(u, wi, o_qb, gain_b, cos_f, sin_f, SM_SCALE * LOG2E)
        gates, o_groups, lse_groups = _gates_and_dilated(u, wi, o_gates, qa, ka, va)
        out_b = _gqa(qb, kb, vb)
        h = _merge(o_groups, lse_groups, out_b, gates, wa, wb, wo, h, mod4, 5)
        (h,) = _ffn(h, mod4, norm_ffn2[l], w1b, w3b, w2b, (6, 7, 8), "final", norm_final)
    return h
```

```python
import functools

import jax
import jax.numpy as jnp
from jax import lax
from jax.experimental import pallas as pl
from jax.experimental.pallas import tpu as pltpu

F32 = jnp.float32
BF16 = jnp.bfloat16

D_MODEL = 2048
D_FF = 5632
HEAD_DIM = 128
DIL_GROUPS = ((128, 1), (512, 4), (2048, 16))
N_DIL_GROUPS = 3
HEADS_PER_DIL = 4
N_HEADS_A = N_DIL_GROUPS * HEADS_PER_DIL
N_Q_B = 8
N_KV_B = 2
Q_PER_KV = N_Q_B // N_KV_B
GRID_W = 64
ROPE_THETA = 10000.0
ALIBI_MAX = 8.0
N_MOD = 9
EPS = 1e-6
D_A = N_HEADS_A * HEAD_DIM
D_A_OUT = HEADS_PER_DIL * HEAD_DIM
D_QB = N_Q_B * HEAD_DIM
D_KVB = N_KV_B * HEAD_DIM
SM_SCALE = HEAD_DIM ** -0.5
MASK_VALUE = -1e30

LANES = 128
BF16_SUBLANES = 16
VMEM_LIMIT = 56 * 1024 * 1024


def _params(sem):
    return pltpu.CompilerParams(dimension_semantics=sem, vmem_limit_bytes=VMEM_LIMIT)


def _silu(x):
    return x * jax.nn.sigmoid(x)


def _rms(x, gain):
    ms = jnp.mean(x * x, axis=-1, keepdims=True)
    return x * lax.rsqrt(ms + EPS) * gain


def _rms_head(x, gain):
    sq = x * x
    hi = sq.astype(BF16)
    lo = (sq - hi.astype(F32)).astype(BF16)
    ones = jnp.ones((HEAD_DIM, HEAD_DIM), BF16)
    ssq = (jnp.dot(hi, ones, preferred_element_type=F32) + jnp.dot(lo, ones, preferred_element_type=F32))
    return x * lax.rsqrt(ssq * (1.0 / HEAD_DIM) + EPS) * gain


ADA_TN = 2048


def _ada_kernel(c_ref, w_ref, b_ref, o_ref):
    n_b = c_ref.shape[0]
    for b in range(n_b):
        cs = _silu(c_ref[b])
        for j in range(ADA_TN // LANES):
            sl = slice(j * LANES, (j + 1) * LANES)
            acc = jnp.sum(w_ref[:, sl] * cs, axis=0, keepdims=True)
            o_ref[b, :, sl] = acc + b_ref[:, sl]


def _ada(c, w, b):
    n_b, d = c.shape
    n = w.shape[1]
    c_b = jnp.broadcast_to(c[:, :, None], (n_b, d, LANES))
    return pl.pallas_call(
        _ada_kernel,
        grid=(n // ADA_TN,),
        in_specs=[
            pl.BlockSpec((n_b, d, LANES), lambda j: (0, 0, 0)),
            pl.BlockSpec((d, ADA_TN), lambda j: (0, j)),
            pl.BlockSpec((1, ADA_TN), lambda j: (0, j)),
        ],
        out_specs=pl.BlockSpec((n_b, 1, ADA_TN), lambda j: (0, 0, j)),
        out_shape=jax.ShapeDtypeStruct((n_b, 1, n), F32),
        compiler_params=_params(("arbitrary",)),
        name="ada_mod",
    )(c_b, w, b.reshape(1, n))


FFN_TM = 512
FFN_TF = 512
FFN_FIRST_ROWS = (256, 256)
FFN_LAST_ROWS = (256, 256)


def _ffn_kernel(x_ref, gain_ref, sh_ref, sc_ref, g_ref, w1_ref, w3_ref, w2_ref, *rest, mode, n_cast):
    if mode == "mid":
        gain2_ref, sh2_ref, sc2_ref = rest[:3]
        f32_refs, (h_ref, u2_ref) = rest[3:3 + n_cast], rest[3 + n_cast:5 + n_cast]
        bf16_refs, (u_scr, acc_scr) = rest[5 + n_cast:5 + 2 * n_cast], rest[5 + 2 * n_cast:]
    else:
        gainf_ref = rest[0]
        f32_refs, y_ref = rest[1:1 + n_cast], rest[1 + n_cast]
        bf16_refs, (u_scr, acc_scr) = rest[2 + n_cast:2 + 2 * n_cast], rest[2 + 2 * n_cast:]
    for src, dst in zip(f32_refs, bf16_refs):
        dst[...] = src[...].astype(BF16)
    f = pl.program_id(2)
    nf = pl.num_programs(2)
    def chunks(sizes):
        starts = [sum(sizes[:k]) for k in range(len(sizes))]
        return [slice(r, r + n) for r, n in zip(starts, sizes)]

    def swiglu_part(u):
        a = jnp.dot(u, w1_ref[...], preferred_element_type=F32)
        b = jnp.dot(u, w3_ref[...], preferred_element_type=F32)
        hm = ((0.5 * a) * (1.0 + jnp.tanh(0.5 * a)) * b).astype(BF16)
        return jnp.dot(hm, w2_ref[...], preferred_element_type=F32)

    @pl.when(f == 0)
    def _():
        gain_mod = gain_ref[...] * (1.0 + sc_ref[0, 0])
        for rows in chunks(FFN_FIRST_ROWS):
            u = (_rms(x_ref[0, rows], gain_mod) + sh_ref[0, 0]).astype(BF16)
            u_scr[rows] = u
            acc_scr[rows] = swiglu_part(u)

    @pl.when((f > 0) & (f < nf - 1))
    def _():
        acc_scr[...] += swiglu_part(u_scr[...])

    @pl.when(f == nf - 1)
    def _():
        if mode == "mid":
            gain2_mod = gain2_ref[...] * (1.0 + sc2_ref[0, 0])
        for rows in chunks(FFN_LAST_ROWS):
            total = acc_scr[rows] + swiglu_part(u_scr[rows])
            h = x_ref[0, rows] + (0.5 * g_ref[0, 0]) * total
            if mode == "mid":
                h_ref[0, rows] = h
                u2_ref[0, rows] = (_rms(h, gain2_mod) + sh2_ref[0, 0]).astype(BF16)
            else:
                y_ref[0, rows] = _rms(h, gainf_ref[...])


def _mod_spec(k):
    return pl.BlockSpec((1, 1, 1, D_MODEL), lambda b, i, f, k=k: (b, k, 0, 0))


def _ffn_cast_spec(shape, n_b, n_i, n_f):
    rows = shape[0] // (n_b * n_i)
    assert rows * n_b * n_i == shape[0] and rows % BF16_SUBLANES == 0
    lane_blocks = shape[1] // LANES
    n_col = max(k for k in range(1, n_f + 1) if lane_blocks % k == 0)
    return pl.BlockSpec((rows, shape[1] // n_col),
                        lambda b, i, f: (b * n_i + i, jnp.minimum(f, n_col - 1)))


def _ffn(x, mod4, gain, w1, w3, w2, ks, mode, gain_next, ks_next=None, to_cast=()):
    n_b, s, d = x.shape
    tm, tf = FFN_TM, FFN_TF
    grid = (n_b, s // tm, D_FF // tf)
    cast_specs = [_ffn_cast_spec(arr.shape, *grid) for arr in to_cast]
    cast_shapes = [jax.ShapeDtypeStruct(arr.shape, BF16) for arr in to_cast]
    row_spec = pl.BlockSpec((1, tm, d), lambda b, i, f: (b, i, 0))
    vec_spec = pl.BlockSpec((1, d), lambda b, i, f: (0, 0))
    in_specs = [
        row_spec, vec_spec, _mod_spec(ks[0]), _mod_spec(ks[1]), _mod_spec(ks[2]),
        pl.BlockSpec((d, tf), lambda b, i, f: (0, f)),
        pl.BlockSpec((d, tf), lambda b, i, f: (0, f)),
        pl.BlockSpec((tf, d), lambda b, i, f: (f, 0)),
        vec_spec,
    ]
    args = [x, gain.reshape(1, d), mod4, mod4, mod4, w1, w3, w2, gain_next.reshape(1, d)]
    if mode == "mid":
        in_specs += [_mod_spec(ks_next[0]), _mod_spec(ks_next[1])]
        args += [mod4, mod4]
        out_specs = [row_spec, row_spec]
        out_shape = [jax.ShapeDtypeStruct((n_b, s, d), F32), jax.ShapeDtypeStruct((n_b, s, d), BF16)]
    else:
        out_specs = [row_spec]
        out_shape = [jax.ShapeDtypeStruct((n_b, s, d), F32)]
    return pl.pallas_call(
        functools.partial(_ffn_kernel, mode=mode, n_cast=len(to_cast)),
        grid=grid,
        in_specs=in_specs + cast_specs,
        out_specs=out_specs + cast_specs,
        out_shape=out_shape + cast_shapes,
        scratch_shapes=[pltpu.VMEM((tm, d), BF16), pltpu.VMEM((tm, d), F32)],
        compiler_params=_params(("arbitrary", "arbitrary", "arbitrary")),
        name="ffn_" + mode,
    )(*args, *to_cast)


PROJ_TM = 1024


def _proj_b_kernel(u_ref, w_ref, gain_ref, cos_ref, sin_ref, q_ref, k_ref, vt_ref, *, q_scale):
    acc = jnp.dot(u_ref[0], w_ref[...], preferred_element_type=F32)
    tm = acc.shape[0]
    for j in range(N_Q_B + N_KV_B):
        sl = slice(j * HEAD_DIM, (j + 1) * HEAD_DIM)
        y = _rms_head(acc[:, sl], gain_ref[:, sl])
        y = y * cos_ref[...] + pltpu.roll(y, HEAD_DIM // 2, axis=1) * sin_ref[...]
        if j < N_Q_B:
            q_ref[0, :, sl] = (y * q_scale).astype(BF16)
        else:
            k_ref[0, :, (j - N_Q_B) * HEAD_DIM:(j - N_Q_B + 1) * HEAD_DIM] = y.astype(BF16)
    v_t = acc[:, D_QB + D_KVB:].T.astype(BF16)
    for hh in range(N_KV_B):
        for cc in range(tm // GQA_TK):
            vt_ref[0, hh, cc] = v_t[hh * HEAD_DIM:(hh + 1) * HEAD_DIM, cc * GQA_TK:(cc + 1) * GQA_TK]


def _proj_b(u, w, col0, gain_row, cos, sin, q_scale):
    n_b, s, d = u.shape
    tm = PROJ_TM
    n = D_QB + 2 * D_KVB
    assert col0 % n == 0
    return pl.pallas_call(
        functools.partial(_proj_b_kernel, q_scale=q_scale),
        grid=(n_b, s // tm),
        in_specs=[
            pl.BlockSpec((1, tm, d), lambda b, i: (b, i, 0)),
            pl.BlockSpec((d, n), lambda b, i: (0, col0 // n)),
            pl.BlockSpec((1, D_QB + D_KVB), lambda b, i: (0, 0)),
            pl.BlockSpec((tm, HEAD_DIM), lambda b, i: (i, 0)),
            pl.BlockSpec((tm, HEAD_DIM), lambda b, i: (i, 0)),
        ],
        out_specs=[
            pl.BlockSpec((1, tm, D_QB), lambda b, i: (b, i, 0)),
            pl.BlockSpec((1, tm, D_KVB), lambda b, i: (b, i, 0)),
            pl.BlockSpec((1, N_KV_B, tm // GQA_TK, HEAD_DIM, GQA_TK), lambda b, i: (b, 0, i, 0, 0)),
        ],
        out_shape=[
            jax.ShapeDtypeStruct((n_b, s, D_QB), BF16),
            jax.ShapeDtypeStruct((n_b, s, D_KVB), BF16),
            jax.ShapeDtypeStruct((n_b, N_KV_B, s // GQA_TK, HEAD_DIM, GQA_TK), BF16),
        ],
        compiler_params=_params(("parallel", "arbitrary")),
        name="proj_mixer_b",
    )(u, w, gain_row, cos, sin)


def _proj_classes_kernel(u_ref, w_ref, *rest, norm, scale):
    if norm:
        gain_ref, rest = rest[0], rest[1:]
    f32_ref, rest = rest[0], rest[1:]
    o_refs, bf16_ref, scr = rest[:N_DIL_GROUPS], rest[N_DIL_GROUPS], rest[N_DIL_GROUPS + 1]
    bf16_ref[...] = f32_ref[...].astype(BF16)
    acc = jnp.dot(u_ref[0], w_ref[...], preferred_element_type=F32)
    tm = acc.shape[0]
    for g, (_, dil) in enumerate(DIL_GROUPS):
        for h in range(HEADS_PER_DIL):
            head = g * HEADS_PER_DIL + h
            lanes = slice(h * HEAD_DIM, (h + 1) * HEAD_DIM)
            y = acc[:, head * HEAD_DIM:(head + 1) * HEAD_DIM]
            if norm:
                y = _rms_head(y, gain_ref[...])
            if scale != 1.0:
                y = y * scale
            if dil == 1:
                o_refs[g][0, 0, :, lanes] = y.astype(BF16)
                continue
            scr[head] = y
            for r in range(dil):
                o_refs[g][0, r, :, lanes] = scr[head, pl.ds(r, tm // dil, stride=dil), :].astype(BF16)


def _proj_classes(u, w, which, to_cast, gain=None, scale=1.0):
    n_b, s, d = u.shape
    tm = PROJ_TM
    n_i = s // tm
    norm = gain is not None
    in_specs = [
        pl.BlockSpec((1, tm, d), lambda b, i: (b, i, 0)),
        pl.BlockSpec((d, D_A), lambda b, i: (0, which)),
    ]
    args = [u, w]
    if norm:
        in_specs.append(pl.BlockSpec((1, HEAD_DIM), lambda b, i: (0, 0)))
        args.append(gain.reshape(1, HEAD_DIM))
    cast_rows = to_cast.shape[0] // (n_b * n_i)
    assert cast_rows * n_b * n_i == to_cast.shape[0] and cast_rows % BF16_SUBLANES == 0
    cast_spec = pl.BlockSpec((cast_rows, to_cast.shape[1]), lambda b, i: (b * n_i + i, 0))
    out = pl.pallas_call(
        functools.partial(_proj_classes_kernel, norm=norm, scale=scale),
        grid=(n_b, n_i),
        in_specs=in_specs + [cast_spec],
        out_specs=[pl.BlockSpec((1, dil, tm // dil, D_A_OUT), lambda b, i: (b, 0, i, 0))
                   for _, dil in DIL_GROUPS] + [cast_spec],
        out_shape=[jax.ShapeDtypeStruct((n_b, dil, s // dil, D_A_OUT), BF16) for _, dil in DIL_GROUPS]
        + [jax.ShapeDtypeStruct(to_cast.shape, BF16)],
        scratch_shapes=[pltpu.VMEM((N_HEADS_A, tm, HEAD_DIM), F32)],
        compiler_params=_params(("parallel", "arbitrary")),
        name="proj_classes",
    )(*args, to_cast)
    return out[:N_DIL_GROUPS], out[N_DIL_GROUPS]


GQA_TQ = 512
GQA_TK = 512
LOG2E = 1.4426950408889634


def _gqa_kernel(q_ref, qn_ref, k_ref, vt_ref, o_ref, acc_ref, s_ref):
    n_chunks = vt_ref.shape[2]
    tq = q_ref.shape[1]
    heads = range(Q_PER_KV)
    nt_dims = (((1,), (1,)), ((), ()))

    def k_chunk(c):
        return k_ref[0, pl.ds(pl.multiple_of(c * GQA_TK, GQA_TK), GQA_TK), :]

    def q_head(g, ref=q_ref):
        return ref[0, :, g * HEAD_DIM:(g + 1) * HEAD_DIM]

    ones_rows = jnp.ones((BF16_SUBLANES, GQA_TK), BF16)

    def step(c, c_next, slot, ms, next_q_ref=q_ref):
        k_next = k_chunk(c_next)
        vt = jnp.concatenate([vt_ref[0, 0, c], ones_rows], axis=0)
        new_ms = []
        for g in heads:
            s_ref[1 - slot, g] = lax.dot_general(k_next, q_head(g, next_q_ref), nt_dims,
                                                 preferred_element_type=F32)
            m_new = jnp.maximum(ms[g], jnp.max(s_ref[slot, g], axis=0, keepdims=True))
            alpha = jnp.exp2(ms[g] - m_new)
            p = jnp.exp2(s_ref[slot, g] - m_new).astype(BF16)
            new_ms.append(m_new)
            acc_ref[g] = alpha * acc_ref[g] + jnp.dot(vt, p, preferred_element_type=F32)
        return tuple(new_ms)

    acc_ref[...] = jnp.zeros(acc_ref.shape, F32)

    @pl.when(pl.program_id(2) == 0)
    def _():
        k0 = k_chunk(0)
        for g in heads:
            s_ref[0, g] = lax.dot_general(k0, q_head(g), nt_dims, preferred_element_type=F32)

    def body(i, ms):
        ms = step(2 * i, 2 * i + 1, 0, ms)
        return step(2 * i + 1, 2 * i + 2, 1, ms)

    init = tuple(jnp.full((1, tq), MASK_VALUE, F32) for _ in heads)
    ms = lax.fori_loop(0, n_chunks // 2 - 1, body, init)
    ms = step(n_chunks - 2, n_chunks - 1, 0, ms)
    step(n_chunks - 1, 0, 1, ms, qn_ref)
    for g in heads:
        o_t = acc_ref[g, :HEAD_DIM] / acc_ref[g, HEAD_DIM:HEAD_DIM + 1]
        o_ref[0, :, g * HEAD_DIM:(g + 1) * HEAD_DIM] = o_t.T.astype(BF16)


def _gqa(qb, kb, vbt):
    n_b, s, _ = qb.shape
    tq = GQA_TQ
    w = Q_PER_KV * HEAD_DIM
    nq = s // tq
    return pl.pallas_call(
        _gqa_kernel,
        grid=(n_b, N_KV_B, nq),
        in_specs=[
            pl.BlockSpec((1, tq, w), lambda b, h, i: (b, i, h)),
            pl.BlockSpec((1, tq, w), lambda b, h, i: (b, jnp.minimum(i + 1, nq - 1), h)),
            pl.BlockSpec((1, s, HEAD_DIM), lambda b, h, i: (b, 0, h)),
            pl.BlockSpec((1, 1, s // GQA_TK, HEAD_DIM, GQA_TK), lambda b, h, i: (b, h, 0, 0, 0)),
        ],
        out_specs=pl.BlockSpec((1, tq, w), lambda b, h, i: (b, i, h)),
        out_shape=jax.ShapeDtypeStruct((n_b, s, D_QB), BF16),
        scratch_shapes=[pltpu.VMEM((Q_PER_KV, HEAD_DIM + BF16_SUBLANES, tq), F32),
                        pltpu.VMEM((2, Q_PER_KV, GQA_TK, tq), F32)],
        compiler_params=_params(("arbitrary", "arbitrary", "arbitrary")),
        name="gqa_attn",
    )(qb, qb, kb, vbt)


DIL_TQ = 512
DIL_SUB = 128
DIL_HALF = 64


def _dil_tile(q_ref, kp_ref, kc_ref, kn_ref, vp_ref, vc_ref, vn_ref, o_ref, lse_ref, i, heads, *,
              group, dil, n_rows):
    win = DIL_SUB + 2 * DIL_HALF
    row = lax.broadcasted_iota(jnp.int32, (DIL_SUB, win), 0)
    col = lax.broadcasted_iota(jnp.int32, (DIL_SUB, win), 1)
    dist = jnp.abs(col - DIL_HALF - row)
    in_band = dist <= DIL_HALF
    dist_f = dist.astype(F32) * float(dil)
    for h in heads:
        sl = slice(h * HEAD_DIM, (h + 1) * HEAD_DIM)
        slope = 2.0 ** (-ALIBI_MAX * (group * HEADS_PER_DIL + h + 1) / N_HEADS_A)
        k = jnp.concatenate([kp_ref[:, sl], kc_ref[:, sl], kn_ref[:, sl]], axis=0)
        v = jnp.concatenate([vp_ref[:, sl], vc_ref[:, sl], vn_ref[:, sl]], axis=0)
        for j in range(DIL_TQ // DIL_SUB):
            rows = slice(j * DIL_SUB, (j + 1) * DIL_SUB)
            k_abs = i * DIL_TQ + (j * DIL_SUB - DIL_HALF) + col
            valid = in_band & (k_abs >= 0) & (k_abs < n_rows)
            s = lax.dot_general(q_ref[rows, sl], k[j * DIL_SUB:j * DIL_SUB + win],
                                (((1,), (1,)), ((), ())), preferred_element_type=F32)
            s = jnp.where(valid, s - slope * dist_f, MASK_VALUE)
            m = jnp.max(s, axis=-1, keepdims=True)
            p = jnp.exp(s - m)
            l = jnp.sum(p, axis=-1, keepdims=True)
            o = jnp.dot(p.astype(BF16), v[j * DIL_SUB:j * DIL_SUB + win], preferred_element_type=F32)
            o_ref[rows, sl] = (o / l).astype(BF16)
            lse_ref[rows, sl] = jnp.broadcast_to(m + jnp.log(l), (DIL_SUB, HEAD_DIM))


GATES_TN = 2048
DIL_REFS = 7


def _gates_dilated_kernel(u_ref, w_ref, *rest, seq_len):
    n_in = DIL_REFS * N_DIL_GROUPS
    dil_in, gates_ref, dil_out = rest[:n_in], rest[n_in], rest[n_in + 1:]
    t = pl.program_id(1) * pl.num_programs(2) + pl.program_id(2)
    rows_per_head = u_ref.shape[1] // HEADS_PER_DIL
    for h in range(HEADS_PER_DIL):
        rows = slice(h * rows_per_head, (h + 1) * rows_per_head)
        half = w_ref.shape[1] // 2
        for g, (_, dil) in enumerate(DIL_GROUPS):
            n_rows = seq_len // dil
            _dil_tile(*dil_in[DIL_REFS * g:DIL_REFS * (g + 1)], dil_out[2 * g], dil_out[2 * g + 1],
                      t % (n_rows // DIL_TQ), (h,), group=g, dil=dil, n_rows=n_rows)
            if g < 2:
                cols = slice(g * half, (g + 1) * half)
                gates_ref[0, rows, cols] = jax.nn.sigmoid(
                    jnp.dot(u_ref[0, rows], w_ref[:, cols], preferred_element_type=F32)).astype(BF16)


def _gates_and_dilated(u, w, col0, qa, ka, va):
    n_b, s, d = u.shape
    tm, tn = PROJ_TM, GATES_TN
    n_i, n_j = s // tm, 2 * d // tn
    assert col0 % tn == 0
    in_specs = [
        pl.BlockSpec((1, tm, d), lambda b, j, i: (b, i, 0)),
        pl.BlockSpec((d, tn), lambda b, j, i: (0, col0 // tn + j), pipeline_mode=pl.Buffered(1)),
    ]
    args = [u, w]
    out_specs = [pl.BlockSpec((1, tm, tn), lambda b, j, i: (b, i, j))]
    out_shape = [jax.ShapeDtypeStruct((n_b, s, 2 * d), BF16)]
    for g, (_, dil) in enumerate(DIL_GROUPS):
        n_rows = s // dil
        per_class = n_rows // DIL_TQ
        assert dil * per_class == n_j * n_i
        halo_per_blk = DIL_TQ // DIL_HALF
        n_halo = n_rows // DIL_HALF

        def tile(j, i, per_class=per_class):
            t = j * n_i + i
            return t // per_class, t % per_class

        def cur_map(b, j, i, tile=tile):
            r, k = tile(j, i)
            return (b, r, k, 0)

        def prev_map(b, j, i, tile=tile, hpb=halo_per_blk):
            r, k = tile(j, i)
            return (b, r, jnp.maximum(k * hpb - 1, 0), 0)

        def next_map(b, j, i, tile=tile, hpb=halo_per_blk, n_halo=n_halo):
            r, k = tile(j, i)
            return (b, r, jnp.minimum((k + 1) * hpb, n_halo - 1), 0)

        cur = pl.BlockSpec((None, None, DIL_TQ, D_A_OUT), cur_map)
        prev = pl.BlockSpec((None, None, DIL_HALF, D_A_OUT), prev_map)
        nxt = pl.BlockSpec((None, None, DIL_HALF, D_A_OUT), next_map)
        in_specs += [cur, prev, cur, nxt, prev, cur, nxt]
        args += [qa[g], ka[g], ka[g], ka[g], va[g], va[g], va[g]]
        out_specs += [cur, cur]
        out_shape += [jax.ShapeDtypeStruct((n_b, dil, n_rows, D_A_OUT), BF16),
                      jax.ShapeDtypeStruct((n_b, dil, n_rows, D_A_OUT), F32)]
    out = pl.pallas_call(
        functools.partial(_gates_dilated_kernel, seq_len=s),
        grid=(n_b, n_j, n_i),
        in_specs=in_specs,
        out_specs=out_specs,
        out_shape=out_shape,
        compiler_params=_params(("arbitrary", "arbitrary", "arbitrary")),
        name="gates_dilated",
    )(*args)
    return out[0], out[1::2], out[2::2]


MERGE_TM = 256


def _merge_kernel(o0_ref, o1_ref, o2_ref, l0_ref, l1_ref, l2_ref, ob_ref, sa_ref, sb_ref,
                  wa_ref, wb_ref, wo_ref, h_ref, g_ref, out_ref, nat_ref):
    tm = out_ref.shape[1]

    def natural(ref, dil, slot):
        if dil == 1:
            return ref[0].astype(F32)
        for r in range(dil):
            for h in range(HEADS_PER_DIL):
                nat_ref[slot, h, pl.ds(r, tm // dil, stride=dil), :] = (
                    ref[r, :, h * HEAD_DIM:(h + 1) * HEAD_DIM].astype(F32))
        return jnp.concatenate([nat_ref[slot, h] for h in range(HEADS_PER_DIL)], axis=1)

    gated_b = sb_ref[0].astype(F32) * jnp.dot(ob_ref[0], wb_ref[...], preferred_element_type=F32)
    dils = [dil for _, dil in DIL_GROUPS]
    l0, l1, l2 = [natural(ref, dil, n) for n, (ref, dil) in enumerate(zip((l0_ref, l1_ref, l2_ref), dils))]
    o0, o1, o2 = [natural(ref, dil, N_DIL_GROUPS + n)
                  for n, (ref, dil) in enumerate(zip((o0_ref, o1_ref, o2_ref), dils))]
    mx = jnp.maximum(jnp.maximum(l0, l1), l2)
    e0, e1, e2 = jnp.exp(l0 - mx), jnp.exp(l1 - mx), jnp.exp(l2 - mx)
    oa = (e0 * o0 + e1 * o1 + e2 * o2) / (e0 + e1 + e2)
    pa = jnp.dot(oa.astype(BF16), wa_ref[...], preferred_element_type=F32)
    merged = sa_ref[0].astype(F32) * pa + gated_b
    mixed = jnp.dot(merged.astype(BF16), wo_ref[...], preferred_element_type=F32)
    out_ref[0] = h_ref[0] + g_ref[0, 0] * mixed


def _merge(o_groups, lse_groups, out_b, gates, wa, wb, wo, h, mod4, k_gate):
    n_b, s, d = h.shape
    tm = MERGE_TM
    a_specs = [pl.BlockSpec((None, dil, tm // dil, D_A_OUT), lambda b, i: (b, 0, i, 0))
               for _, dil in DIL_GROUPS]
    row_spec = pl.BlockSpec((1, tm, d), lambda b, i: (b, i, 0))
    full = lambda shape: pl.BlockSpec(shape, lambda b, i: (0, 0))
    return pl.pallas_call(
        _merge_kernel,
        grid=(n_b, s // tm),
        in_specs=a_specs + a_specs + [
            pl.BlockSpec((1, tm, D_QB), lambda b, i: (b, i, 0)),
            pl.BlockSpec((1, tm, d), lambda b, i: (b, i, 0)),
            pl.BlockSpec((1, tm, d), lambda b, i: (b, i, 1)),
            full((D_A_OUT, d)), full((D_QB, d)), full((d, d)),
            row_spec,
            pl.BlockSpec((1, 1, 1, d), lambda b, i: (b, k_gate, 0, 0)),
        ],
        out_specs=row_spec,
        out_shape=jax.ShapeDtypeStruct((n_b, s, d), F32),
        scratch_shapes=[pltpu.VMEM((2 * N_DIL_GROUPS, HEADS_PER_DIL, tm, HEAD_DIM), F32)],
        compiler_params=_params(("parallel", "arbitrary")),
        name="mix_merge",
    )(*o_groups, *lse_groups, out_b, gates, gates, wa, wb, wo, h, mod4)


def _rope_tables(s):
    rows = s // GRID_W
    half = HEAD_DIM // 2
    inv_freq = ROPE_THETA ** (-jnp.arange(0, half, 2, dtype=F32) / half)
    ang_row = jnp.arange(rows).astype(F32)[:, None] * inv_freq
    ang_col = jnp.arange(GRID_W).astype(F32)[:, None] * inv_freq

    def expand(fn):
        return jnp.concatenate([jnp.repeat(fn(ang_row), GRID_W, axis=0),
                                jnp.tile(fn(ang_col), (rows, 1))], axis=-1)

    cos, sin = expand(jnp.cos), expand(jnp.sin)
    return jnp.concatenate([cos, cos], axis=-1), jnp.concatenate([-sin, sin], axis=-1)


def kernel(x, c, w_ada, b_ada, norm_ffn1, w1_ffn1, w3_ffn1, w2_ffn1, norm_mix, w_in, q_norm_a, k_norm_a, q_norm_b, k_norm_b, w_branch_a, w_branch_b, w_out, norm_ffn2, w1_ffn2, w3_ffn2, w2_ffn2, norm_final):
    n_b, s, d = x.shape
    assert w_ada.shape[0] == 1, "the fused pipeline covers a single layer"
    cos_f, sin_f = _rope_tables(s)
    h = x
    for l in range(1):
        mod4 = _ada(c, w_ada[l], b_ada[l]).reshape(n_b, N_MOD, 1, d)
        h, u, wi, wa, wb, wo = _ffn(
            h, mod4, norm_ffn1[l], w1_ffn1[l].astype(BF16), w3_ffn1[l].astype(BF16),
            w2_ffn1[l].astype(BF16), (0, 1, 2), "mid", norm_mix[l], (3, 4),
            to_cast=[w_in[l], w_branch_a[l], w_branch_b[l], w_out[l]])
        o_qb = 3 * D_A
        o_gates = o_qb + D_QB + 2 * D_KVB
        gain_b = jnp.concatenate([jnp.tile(q_norm_b[l], N_Q_B), jnp.tile(k_norm_b[l], N_KV_B)])[None]
        qa, w1b = _proj_classes(u, wi, 0, w1_ffn2[l], q_norm_a[l], scale=SM_SCALE)
        ka, w3b = _proj_classes(u, wi, 1, w3_ffn2[l], k_norm_a[l])
        va, w2b = _proj_classes(u, wi, 2, w2_ffn2[l])
        qb, kb, vb = _proj_b(u, wi, o_qb, gain_b, cos_f, sin_f, SM_SCALE * LOG2E)
        gates, o_groups, lse_groups = _gates_and_dilated(u, wi, o_gates, qa, ka, va)
        out_b = _gqa(qb, kb, vb)
        h = _merge(o_groups, lse_groups, out_b, gates, wa, wb, wo, h, mod4, 5)
        (h,) = _ffn(h, mod4, norm_ffn2[l], w1b, w3b, w2b, (6, 7, 8), "final", norm_final)
    return h
```
